```python
import math
import jax
import jax.numpy as jnp
from jax import lax
import numpy as np

D_MODEL = 1024
BATCH = 8
SEQ = 4096
DEPTH = 1
DEC_BATCH = 128
DEC_SEQ = 4
PAST_LEN = 8192
PAGE_SIZE = 128

HEAD_DIM = 64
MIX_WIDTH = D_MODEL
N_HEADS_A = (MIX_WIDTH // 2) // HEAD_DIM
N_HEADS_B = (MIX_WIDTH // 2) // (2 * HEAD_DIM)
DV_B = 2 * HEAD_DIM
A_WIDTH = N_HEADS_A * HEAD_DIM
B_WIDTH = N_HEADS_B * 2 * HEAD_DIM
N_IDX_HEADS = 8
D_IDX = 64
TOPK_MAX = 256
N_MEM = 256
N_HEADS_MEM = 4
HEAD_DIM_MEM = 64
N_BUCKETS = 32
MAX_EXACT = N_BUCKETS // 2
MAX_DISTANCE = 128
N_BIAS_HEADS = N_HEADS_A + 2 * N_HEADS_B
N_GROUPS = 4
EXPERTS_PER_GROUP = 8
N_EXPERTS = N_GROUPS * EXPERTS_PER_GROUP
TOP_K_INNER = 2
D_EXPERT = 512
EXPERT_BLOCK = 128
Q_BLOCK = 128
NEG = -1e30
EPS = 1e-6

kernel_name = 'hymba_dsa_diffattn_hmoe_step'


def rmsnorm(x, g):
    xf = x.astype(jnp.float32)
    y = xf * lax.rsqrt(jnp.mean(xf * xf, axis=-1, keepdims=True) + EPS) * g.astype(jnp.float32)
    return y.astype(x.dtype)


def t5_bucket(dist):
    n = jnp.maximum(dist, 0)
    nf = jnp.maximum(n, 1).astype(jnp.float32)
    large = MAX_EXACT + (jnp.log(nf / MAX_EXACT) / math.log(MAX_DISTANCE / MAX_EXACT)
                         * (N_BUCKETS - MAX_EXACT)).astype(jnp.int32)
    large = jnp.minimum(large, N_BUCKETS - 1)
    return jnp.where(n < MAX_EXACT, n, large)


def take_rows(arr, idx):
    return jax.vmap(lambda a, i: a[i])(arr, idx)


def project_mixers(h, w_in, g_kidx):
    b, t, _ = h.shape
    z = h @ w_in
    sizes = (A_WIDTH, A_WIDTH, A_WIDTH, N_IDX_HEADS * D_IDX, D_IDX, N_IDX_HEADS, B_WIDTH, B_WIDTH, B_WIDTH)
    offs = [sum(sizes[:i]) for i in range(1, len(sizes))]
    q_a, k_a, v_a, q_i, k_i, w_i, q_b, k_b, v_b = jnp.split(z, offs, axis=-1)
    q_a = q_a.reshape(b, t, N_HEADS_A, HEAD_DIM)
    kv_a = jnp.stack([k_a, v_a], axis=2).reshape(b, t, 2, N_HEADS_A, HEAD_DIM)
    q_i = q_i.reshape(b, t, N_IDX_HEADS, D_IDX)
    k_i = rmsnorm(k_i, g_kidx)
    q_b = q_b.reshape(b, t, N_HEADS_B, 2, HEAD_DIM)
    k_b = k_b.reshape(b, t, N_HEADS_B, 2, HEAD_DIM)
    v_b = v_b.reshape(b, t, N_HEADS_B, DV_B)
    return q_a, kv_a, q_i, k_i, w_i, q_b, k_b, v_b


def index_scores(q_i, w_i, k_i):
    dots = jnp.einsum('bthd,bsd->bths', q_i, k_i).astype(jnp.float32) * D_IDX ** -0.5
    return jnp.einsum('bths,bth->bts', jax.nn.relu(dots), w_i.astype(jnp.float32) * N_IDX_HEADS ** -0.5)


def sparse_attend(q, kv_sel, dist, valid, table_a):
    b, t = q.shape[:2]
    logits = jnp.einsum('bthd,btkhd->bhtk', q, kv_sel[:, :, :, 0]).astype(jnp.float32) * HEAD_DIM ** -0.5
    bias = table_a[t5_bucket(dist)].astype(jnp.float32)
    logits = logits + jnp.transpose(bias, (0, 3, 1, 2))
    logits = jnp.where(valid[:, None], logits, NEG)
    p = jax.nn.softmax(logits, axis=-1).astype(kv_sel.dtype)
    o = jnp.einsum('bhtk,btkhd->bthd', p, kv_sel[:, :, :, 1])
    return o.reshape(b, t, A_WIDTH)


def dsa_prompt(q_a, kv_a, q_i, k_i, w_i, table_a):
    b, s = q_a.shape[:2]
    k_sel = min(TOPK_MAX, s // 4)
    key_pos = jnp.arange(s, dtype=jnp.int32)

    def block(j):
        q0 = j * Q_BLOCK
        qa = lax.dynamic_slice_in_dim(q_a, q0, Q_BLOCK, axis=1)
        qi = lax.dynamic_slice_in_dim(q_i, q0, Q_BLOCK, axis=1)
        wi = lax.dynamic_slice_in_dim(w_i, q0, Q_BLOCK, axis=1)
        qpos = q0 + jnp.arange(Q_BLOCK, dtype=jnp.int32)
        score = index_scores(qi, wi, k_i)
        score = jnp.where(key_pos[None, None, :] <= qpos[None, :, None], score, NEG)
        _, idx = lax.top_k(score, k_sel)
        kv_sel = take_rows(kv_a, idx)
        dist = qpos[None, :, None] - idx
        return sparse_attend(qa, kv_sel, dist, dist >= 0, table_a)

    o = lax.map(block, jnp.arange(s // Q_BLOCK, dtype=jnp.int32))
    return jnp.transpose(o, (1, 0, 2, 3)).reshape(b, s, A_WIDTH)


def dsa_sample(q_a, kv_new, q_i, k_i_new, w_i, cache_kv_a_l, cache_k_idx_l, page_table, table_a):
    b, t = q_a.shape[:2]
    n_pages = page_table.shape[1]
    past = n_pages * PAGE_SIZE
    n_keys = past + t
    k_sel = min(TOPK_MAX, n_keys // 4)
    k_past = cache_k_idx_l[page_table].reshape(b, past, D_IDX)
    k_all = jnp.concatenate([k_past, k_i_new], axis=1)
    qpos = past + jnp.arange(t, dtype=jnp.int32)
    key_pos = jnp.arange(n_keys, dtype=jnp.int32)
    score = index_scores(q_i, w_i, k_all)
    score = jnp.where(key_pos[None, None, :] <= qpos[None, :, None], score, NEG)
    _, idx = lax.top_k(score, k_sel)
    ip = jnp.minimum(idx, past - 1)
    phys = jax.vmap(lambda pt, i: pt[i])(page_table, ip // PAGE_SIZE)
    kv_past = cache_kv_a_l[phys, ip % PAGE_SIZE]
    kv_cur = take_rows(kv_new, jnp.clip(idx - past, 0, t - 1))
    kv_sel = jnp.where((idx < past)[..., None, None, None], kv_past, kv_cur)
    dist = qpos[None, :, None] - idx
    return sparse_attend(q_a, kv_sel, dist, dist >= 0, table_a)


def diff_logits(q, k, qpos, kpos, table_b):
    logits = jnp.einsum('bthmd,bshmd->bhmts', q, k).astype(jnp.float32) * HEAD_DIM ** -0.5
    dist = qpos[:, None] - kpos[None, :]
    bias = table_b[t5_bucket(dist)].astype(jnp.float32)
    logits = logits + jnp.transpose(bias, (2, 3, 0, 1))[None]
    return jnp.where(dist >= 0, logits, NEG)


def diff_combine(o, lam, lam_init, g_sub, dtype):
    b, _, _, t, _ = o.shape
    o = o.astype(jnp.float32)
    o = o[:, :, 0] - lam * o[:, :, 1]
    o = rmsnorm(o, g_sub) * (1.0 - lam_init)
    return jnp.transpose(o, (0, 2, 1, 3)).reshape(b, t, B_WIDTH).astype(dtype)


def diff_prompt(q_b, k_b, v_b, table_b, lam, lam_init, g_sub):
    b, s = q_b.shape[:2]
    key_pos = jnp.arange(s, dtype=jnp.int32)

    def block(j):
        q0 = j * Q_BLOCK
        q = lax.dynamic_slice_in_dim(q_b, q0, Q_BLOCK, axis=1)
        qpos = q0 + jnp.arange(Q_BLOCK, dtype=jnp.int32)
        p = jax.nn.softmax(diff_logits(q, k_b, qpos, key_pos, table_b), axis=-1).astype(v_b.dtype)
        return jnp.einsum('bhmts,bshe->bhmte', p, v_b)

    o = lax.map(block, jnp.arange(s // Q_BLOCK, dtype=jnp.int32))
    o = jnp.transpose(o, (1, 2, 3, 0, 4, 5)).reshape(b, N_HEADS_B, 2, s, DV_B)
    return diff_combine(o, lam, lam_init, g_sub, q_b.dtype)


def online_update(carry, logits, v):
    m, l, acc = carry
    m_new = jnp.maximum(m, jnp.max(logits, axis=-1))
    p = jnp.exp(logits - m_new[..., None])
    corr = jnp.exp(m - m_new)
    l_new = l * corr + jnp.sum(p, axis=-1)
    acc_new = acc * corr[..., None] + jnp.einsum('bhmts,bshe->bhmte', p, v.astype(jnp.float32))
    return (m_new, l_new, acc_new)


def diff_sample(q_b, k_new, v_new, cache_k_b_l, cache_v_b_l, page_table, table_b, lam, lam_init, g_sub):
    b, t = q_b.shape[:2]
    n_pages = page_table.shape[1]
    past = n_pages * PAGE_SIZE
    qpos = past + jnp.arange(t, dtype=jnp.int32)
    init = (jnp.full((b, N_HEADS_B, 2, t), NEG, jnp.float32),
            jnp.zeros((b, N_HEADS_B, 2, t), jnp.float32),
            jnp.zeros((b, N_HEADS_B, 2, t, DV_B), jnp.float32))

    def step(carry, pg):
        phys = page_table[:, pg]
        k = cache_k_b_l[phys]
        v = cache_v_b_l[phys]
        kpos = pg * PAGE_SIZE + jnp.arange(PAGE_SIZE, dtype=jnp.int32)
        return online_update(carry, diff_logits(q_b, k, qpos, kpos, table_b), v), None

    carry, _ = lax.scan(step, init, jnp.arange(n_pages, dtype=jnp.int32))
    carry = online_update(carry, diff_logits(q_b, k_new, qpos, qpos, table_b), v_new)
    _, l, acc = carry
    return diff_combine(acc / l[..., None], lam, lam_init, g_sub, q_b.dtype)


def memory_kv(mem, g, w_kv):
    b, m, _ = mem.shape
    return (rmsnorm(mem, g) @ w_kv).reshape(b, m, 2, N_HEADS_MEM, HEAD_DIM_MEM)


def mem_attend(h, mkv, w_q, w_o):
    b, t, _ = h.shape
    q = (h @ w_q).reshape(b, t, N_HEADS_MEM, HEAD_DIM_MEM)
    logits = jnp.einsum('bthd,bmhd->bhtm', q, mkv[:, :, 0]).astype(jnp.float32) * HEAD_DIM_MEM ** -0.5
    p = jax.nn.softmax(logits, axis=-1).astype(mkv.dtype)
    o = jnp.einsum('bhtm,bmhd->bthd', p, mkv[:, :, 1]).reshape(b, t, N_HEADS_MEM * HEAD_DIM_MEM)
    return o @ w_o


def grouped_experts(xf, experts, gates, w_gu, w_dn):
    n, d = xf.shape
    n_assign = n * TOP_K_INNER
    e_flat = experts.reshape(n_assign)
    g_flat = gates.reshape(n_assign)
    tok_flat = jnp.arange(n_assign, dtype=jnp.int32) // TOP_K_INNER
    order = jnp.argsort(e_flat)
    e_s, tok_s, g_s = e_flat[order], tok_flat[order], g_flat[order]
    counts = jnp.zeros((N_EXPERTS,), jnp.int32).at[e_flat].add(1)
    starts = jnp.cumsum(counts) - counts
    padded = (counts + EXPERT_BLOCK - 1) // EXPERT_BLOCK * EXPERT_BLOCK
    pad_ends = jnp.cumsum(padded)
    pad_starts = pad_ends - padded
    dest = pad_starts[e_s] + jnp.arange(n_assign, dtype=jnp.int32) - starts[e_s]
    n_blocks = -(-n_assign // EXPERT_BLOCK) + N_EXPERTS
    n_rows = n_blocks * EXPERT_BLOCK
    row_tok = jnp.full((n_rows,), n, jnp.int32).at[dest].set(tok_s)
    row_gate = jnp.zeros((n_rows,), jnp.float32).at[dest].set(g_s)
    block_expert = jnp.minimum(
        jnp.searchsorted(pad_ends, jnp.arange(n_blocks, dtype=jnp.int32) * EXPERT_BLOCK, side='right'),
        N_EXPERTS - 1)
    x_rows = jnp.concatenate([xf, jnp.zeros((1, d), xf.dtype)], axis=0)[row_tok]
    x_rows = x_rows.reshape(n_blocks, EXPERT_BLOCK, d)

    def run_block(args):
        xb, e = args
        gate, up = jnp.split(xb @ w_gu[e], 2, axis=-1)
        return (jax.nn.silu(gate) * up) @ w_dn[e]

    y_rows = lax.map(run_block, (x_rows, block_expert)).reshape(n_rows, d)
    y = jax.ops.segment_sum(y_rows.astype(jnp.float32) * row_gate[:, None], row_tok, num_segments=n + 1)
    return y[:n].astype(xf.dtype)


def moe(h, w_r1, b_r1, w_r2, b_r2, w_gu, w_dn):
    b, t, d = h.shape
    n = b * t
    xf = h.reshape(n, d)
    lg1 = (xf @ w_r1).astype(jnp.float32) + b_r1.astype(jnp.float32)
    p1 = jax.nn.softmax(lg1, axis=-1)
    grp = jnp.argmax(lg1, axis=-1).astype(jnp.int32)
    p_grp = jnp.take_along_axis(p1, grp[:, None], axis=1)[:, 0]
    lg2 = jnp.einsum('nd,gde->nge', xf, w_r2).astype(jnp.float32) + b_r2.astype(jnp.float32)
    lg2 = jnp.take_along_axis(lg2, grp[:, None, None], axis=1)[:, 0]
    top_v, top_i = lax.top_k(lg2, TOP_K_INNER)
    gates = jax.nn.softmax(top_v, axis=-1) * p_grp[:, None]
    experts = grp[:, None] * EXPERTS_PER_GROUP + top_i.astype(jnp.int32)
    return grouped_experts(xf, experts, gates, w_gu, w_dn).reshape(b, t, d)


def residual_tail(x, o_mix, mkv, w_out, g_mem, w_mem_q, w_mem_o, g_ffn, w_r1, b_r1, w_r2, b_r2, w_gu, w_dn):
    x = x + o_mix @ w_out
    x = x + mem_attend(rmsnorm(x, g_mem), mkv, w_mem_q, w_mem_o)
    return x + moe(rmsnorm(x, g_ffn), w_r1, b_r1, w_r2, b_r2, w_gu, w_dn)


def setup_inputs(seed: int = 0) -> dict:
    key = jax.random.key(seed)
    ks = jax.random.split(key, 40)
    f32 = jnp.float32
    n_pages = PAST_LEN // PAGE_SIZE
    n_used = DEC_BATCH * n_pages
    n_phys = n_used + max(1, n_used // 4)

    def nrm(k, shape, scale=1.0):
        return jax.random.normal(k, shape, f32) * scale

    def gain(k, shape):
        return 1.0 + 0.05 * jax.random.normal(k, shape, f32)

    in_cols = 3 * A_WIDTH + N_IDX_HEADS * D_IDX + D_IDX + N_IDX_HEADS + 3 * B_WIDTH
    mem_w = N_HEADS_MEM * HEAD_DIM_MEM
    page_table = jax.random.permutation(ks[0], n_phys)[:n_used].astype(jnp.int32).reshape(DEC_BATCH, n_pages)
    return {
        'x_prompt': nrm(ks[1], (BATCH, SEQ, D_MODEL)),
        'x_sample': nrm(ks[2], (DEC_BATCH, DEC_SEQ, D_MODEL)),
        'cache_kv_a': nrm(ks[3], (DEPTH, n_phys, PAGE_SIZE, 2, N_HEADS_A, HEAD_DIM)),
        'cache_k_idx': nrm(ks[4], (DEPTH, n_phys, PAGE_SIZE, D_IDX)),
        'cache_k_b': nrm(ks[5], (DEPTH, n_phys, PAGE_SIZE, N_HEADS_B, 2, HEAD_DIM)),
        'cache_v_b': nrm(ks[6], (DEPTH, n_phys, PAGE_SIZE, N_HEADS_B, DV_B)),
        'cache_mem_kv': nrm(ks[7], (DEPTH, DEC_BATCH, N_MEM, 2, N_HEADS_MEM, HEAD_DIM_MEM)),
        'page_table': page_table,
        'mem_prompt': nrm(ks[8], (BATCH, N_MEM, D_MODEL)),
        'rel_bias': nrm(ks[9], (N_BUCKETS, N_BIAS_HEADS), 0.5),
        'g_mix': gain(ks[10], (DEPTH, D_MODEL)),
        'w_in': nrm(ks[11], (DEPTH, D_MODEL, in_cols), D_MODEL ** -0.5),
        'g_kidx': gain(ks[12], (DEPTH, D_IDX)),
        'lambda_q1': nrm(ks[13], (DEPTH, HEAD_DIM), 0.1),
        'lambda_k1': nrm(ks[14], (DEPTH, HEAD_DIM), 0.1),
        'lambda_q2': nrm(ks[15], (DEPTH, HEAD_DIM), 0.1),
        'lambda_k2': nrm(ks[16], (DEPTH, HEAD_DIM), 0.1),
        'g_subln': gain(ks[17], (DEPTH, DV_B)),
        'w_out': nrm(ks[18], (DEPTH, MIX_WIDTH, D_MODEL), MIX_WIDTH ** -0.5),
        'g_mem': gain(ks[19], (DEPTH, D_MODEL)),
        'g_memin': gain(ks[20], (DEPTH, D_MODEL)),
        'w_mem_q': nrm(ks[21], (DEPTH, D_MODEL, mem_w), D_MODEL ** -0.5),
        'w_mem_kv': nrm(ks[22], (DEPTH, D_MODEL, 2 * mem_w), D_MODEL ** -0.5),
        'w_mem_o': nrm(ks[23], (DEPTH, mem_w, D_MODEL), mem_w ** -0.5),
        'g_ffn': gain(ks[24], (DEPTH, D_MODEL)),
        'w_router1': nrm(ks[25], (DEPTH, D_MODEL, N_GROUPS), D_MODEL ** -0.5),
        'b_router1': nrm(ks[26], (DEPTH, N_GROUPS), 0.01),
        'w_router2': nrm(ks[27], (DEPTH, N_GROUPS, D_MODEL, EXPERTS_PER_GROUP), D_MODEL ** -0.5),
        'b_router2': nrm(ks[28], (DEPTH, N_GROUPS, EXPERTS_PER_GROUP), 0.01),
        'w_gate_up': nrm(ks[29], (DEPTH, N_EXPERTS, D_MODEL, 2 * D_EXPERT), D_MODEL ** -0.5),
        'w_down': nrm(ks[30], (DEPTH, N_EXPERTS, D_EXPERT, D_MODEL), D_EXPERT ** -0.5),
        'g_final': gain(ks[31], (D_MODEL,)),
    }


def reference(x_prompt, x_sample, cache_kv_a, cache_k_idx, cache_k_b, cache_v_b, cache_mem_kv, page_table,
              mem_prompt, rel_bias, g_mix, w_in, g_kidx, lambda_q1, lambda_k1, lambda_q2, lambda_k2, g_subln,
              w_out, g_mem, g_memin, w_mem_q, w_mem_kv, w_mem_o, g_ffn, w_router1, b_router1, w_router2,
              b_router2, w_gate_up, w_down, g_final):
    table_a = rel_bias[:, :N_HEADS_A]
    table_b = rel_bias[:, N_HEADS_A:].reshape(N_BUCKETS, N_HEADS_B, 2)
    xp, xs = x_prompt, x_sample
    kv_a_p, kidx_p, kb_p, vb_p, mkv_p = [], [], [], [], []
    kv_a_s, kidx_s, kb_s, vb_s = [], [], [], []
    for l in range(DEPTH):
        lam_init = 0.8 - 0.6 * math.exp(-0.3 * l)
        lam = (jnp.exp(jnp.sum(lambda_q1[l] * lambda_k1[l]).astype(jnp.float32))
               - jnp.exp(jnp.sum(lambda_q2[l] * lambda_k2[l]).astype(jnp.float32)) + lam_init)
        tail = (w_out[l], g_mem[l], w_mem_q[l], w_mem_o[l], g_ffn[l], w_router1[l], b_router1[l],
                w_router2[l], b_router2[l], w_gate_up[l], w_down[l])
        h = rmsnorm(xp, g_mix[l])
        q_a, kv_a, q_i, k_i, w_i, q_b, k_b, v_b = project_mixers(h, w_in[l], g_kidx[l])
        o_mix = jnp.concatenate([dsa_prompt(q_a, kv_a, q_i, k_i, w_i, table_a),
                                 diff_prompt(q_b, k_b, v_b, table_b, lam, lam_init, g_subln[l])], axis=-1)
        mkv = memory_kv(mem_prompt, g_memin[l], w_mem_kv[l])
        xp = residual_tail(xp, o_mix, mkv, *tail)
        kv_a_p.append(kv_a)
        kidx_p.append(k_i)
        kb_p.append(k_b)
        vb_p.append(v_b)
        mkv_p.append(mkv)
        h = rmsnorm(xs, g_mix[l])
        q_a, kv_a, q_i, k_i, w_i, q_b, k_b, v_b = project_mixers(h, w_in[l], g_kidx[l])
        o_mix = jnp.concatenate([
            dsa_sample(q_a, kv_a, q_i, k_i, w_i, cache_kv_a[l], cache_k_idx[l], page_table, table_a),
            diff_sample(q_b, k_b, v_b, cache_k_b[l], cache_v_b[l], page_table, table_b, lam, lam_init,
                        g_subln[l])], axis=-1)
        xs = residual_tail(xs, o_mix, cache_mem_kv[l], *tail)
        kv_a_s.append(kv_a)
        kidx_s.append(k_i)
        kb_s.append(k_b)
        vb_s.append(v_b)
    y_prompt = rmsnorm(xp, g_final)
    y_sample = rmsnorm(xs, g_final)
    return (y_prompt, y_sample, jnp.stack(kv_a_p), jnp.stack(kidx_p), jnp.stack(kb_p), jnp.stack(vb_p),
            jnp.stack(mkv_p), jnp.stack(kv_a_s), jnp.stack(kidx_s), jnp.stack(kb_s), jnp.stack(vb_s))
```

```python
import functools
import math

import jax
import jax.numpy as jnp
from jax import lax
from jax.experimental import pallas as pl
from jax.experimental.pallas import tpu as pltpu

F32 = jnp.float32
BF16 = jnp.bfloat16
I32 = jnp.int32

HEAD_DIM = 64
N_HEADS_A = 8
N_HEADS_B = 4
A_WIDTH = 512
B_WIDTH = 512
N_IDX_HEADS = 8
D_IDX = 64
TOPK_MAX = 256
N_HEADS_MEM = 4
N_BUCKETS = 32
MAX_EXACT = 16
MAX_DISTANCE = 128
N_GROUPS = 4
EXPERTS_PER_GROUP = 8
N_EXPERTS = 32
D_EXPERT = 512
NEG = -1e30
EPS = 1e-6
LANES = 128
INT_MIN = -(2 ** 31)

SEG_QA, SEG_KVA, SEG_QI, SEG_KI, SEG_WI, SEG_QB, SEG_KB, SEG_VB = (
    (0, 512), (512, 1536), (1536, 2048), (2048, 2176), (2176, 2304), (2304, 2816), (2816, 3328), (3328, 3840))
PACKED_COLS = 3840

VMEM_LIMIT = 56 * 1024 * 1024
ATTN_TILE = 256
EXPERT_ROWS = 256
GATHER_WINDOW = 32


def _cparams(sem):
    return pltpu.CompilerParams(dimension_semantics=sem, vmem_limit_bytes=VMEM_LIMIT)


def _dot(a, b):
    return jnp.dot(a, b, preferred_element_type=F32)


def _dot_nt(a, b):
    return lax.dot_general(a, b, (((1,), (1,)), ((), ())), preferred_element_type=F32)


def _rms(x, g):
    return x * lax.rsqrt(jnp.mean(x * x, axis=-1, keepdims=True) + EPS) * g


def _fold_lanes(x):
    acc = x[:, :LANES]
    for c in range(1, x.shape[1] // LANES):
        acc = acc + x[:, c * LANES:(c + 1) * LANES]
    return acc


def _sortable_key(score):
    bits = lax.bitcast_convert_type(score, I32)
    key = jnp.where(bits < 0, bits ^ jnp.int32(0x7FFFFFFF), bits)
    return jnp.where(bits == jnp.int32(INT_MIN), 0, key)


def _proj_kernel(x_ref, g_ref, w_ref, gk_ref, qa_ref, kva_ref, kva16_ref, qi_ref, ki_ref, ki16_ref, wi_ref,
                 qb_ref, kb_ref, vb_ref, kb16_ref, vb16_ref):
    h = _rms(x_ref[...], g_ref[...]).astype(BF16)

    def seg(s):
        return _dot(h, w_ref[:, s[0]:s[1]])

    qa_ref[...] = (seg(SEG_QA) * HEAD_DIM ** -0.5).astype(BF16)
    kva = seg(SEG_KVA)
    kva_ref[...] = kva
    kva16_ref[...] = kva.astype(BF16)
    qi_ref[...] = (seg(SEG_QI) * D_IDX ** -0.5).astype(BF16)
    ki = _rms(seg(SEG_KI), gk_ref[...])
    ki_ref[...] = ki
    ki16_ref[...] = ki.astype(BF16)
    wi_ref[...] = seg(SEG_WI) * N_IDX_HEADS ** -0.5
    qb_ref[...] = (seg(SEG_QB) * HEAD_DIM ** -0.5).astype(BF16)
    kb = seg(SEG_KB)
    kb_ref[...] = kb
    kb16_ref[...] = kb.astype(BF16)
    vb = seg(SEG_VB)
    vb_ref[...] = vb
    vb16_ref[...] = vb.astype(BF16)


def _proj(x, g, w_packed, gk2):
    n, d = x.shape
    tm = min(n, 512)
    widths = [(512, BF16), (1024, F32), (1024, BF16), (512, BF16), (128, F32), (128, BF16), (128, F32),
              (512, BF16), (512, F32), (512, F32), (512, BF16), (512, BF16)]
    return pl.pallas_call(
        _proj_kernel,
        grid=(n // tm,),
        in_specs=[pl.BlockSpec((tm, d), lambda i: (i, 0)),
                  pl.BlockSpec((1, d), lambda i: (0, 0)),
                  pl.BlockSpec((d, PACKED_COLS), lambda i: (0, 0)),
                  pl.BlockSpec((1, LANES), lambda i: (0, 0))],
        out_specs=[pl.BlockSpec((tm, w), lambda i: (i, 0)) for w, _ in widths],
        out_shape=[jax.ShapeDtypeStruct((n, w), dt) for w, dt in widths],
        compiler_params=_cparams(("arbitrary",)),
        name="proj",
    )(x, g, w_packed, gk2)


def _norm_matmul_kernel(x_ref, g_ref, w_ref, o_ref):
    o_ref[...] = _dot(_rms(x_ref[...], g_ref[...]).astype(BF16), w_ref[...])


def _norm_matmul(x, g, w16):
    n, d = x.shape
    tm = min(n, 512)
    return pl.pallas_call(
        _norm_matmul_kernel,
        grid=(n // tm,),
        in_specs=[pl.BlockSpec((tm, d), lambda i: (i, 0)),
                  pl.BlockSpec((1, d), lambda i: (0, 0)),
                  pl.BlockSpec(w16.shape, lambda i: (0, 0))],
        out_specs=pl.BlockSpec((tm, w16.shape[1]), lambda i: (i, 0)),
        out_shape=jax.ShapeDtypeStruct((n, w16.shape[1]), F32),
        compiler_params=_cparams(("arbitrary",)),
        name="norm_matmul",
    )(x, g, w16)


def _split_halves(q_ref, dst_ref, n_pairs):
    t = q_ref.shape[0]
    lo = lax.broadcasted_iota(I32, (t, LANES), 1) < HEAD_DIM
    for p in range(n_pairs):
        qp = q_ref[:, p * LANES:(p + 1) * LANES]
        dst_ref[2 * p] = jnp.where(lo, qp, jnp.zeros_like(qp))
        dst_ref[2 * p + 1] = jnp.where(lo, jnp.zeros_like(qp), qp)


def _online_softmax_step(s, h, m_ref, l_ref):
    m_prev = m_ref[h]
    m_new = jnp.maximum(m_prev, jnp.max(s, axis=1, keepdims=True))
    alpha = jnp.exp(m_prev - m_new)
    p = jnp.exp(s - jnp.concatenate([m_new] * (s.shape[1] // LANES), axis=1))
    l_ref[h] = alpha * l_ref[h] + jnp.sum(p, axis=1, keepdims=True)
    m_ref[h] = m_new
    return p, alpha


def _t5_bucket(dist):
    n = jnp.maximum(dist, 0)
    nf = jnp.maximum(n, 1).astype(F32)
    large = MAX_EXACT + (jnp.log(nf / MAX_EXACT) / math.log(MAX_DISTANCE / MAX_EXACT)
                         * (N_BUCKETS - MAX_EXACT)).astype(I32)
    large = jnp.minimum(large, N_BUCKETS - 1)
    return jnp.where(n < MAX_EXACT, n, large)


def _prompt_bias_tiles(table, t):
    assert t >= MAX_DISTANCE
    r = jnp.arange(t, dtype=I32)[:, None]
    c = jnp.arange(t, dtype=I32)[None, :]
    tiles = []
    for delta in range(3):
        dist = delta * t + r - c
        b = jnp.transpose(table[_t5_bucket(dist)].astype(F32), (2, 0, 1))
        tiles.append(jnp.where(dist[None] >= 0, b, NEG))
    return jnp.stack(tiles)


def _dsa_prompt_kernel(k_sel, n_bits, qa_ref, qi_ref, wi_ref, k2_ref, kv_ref, bt_ref, o_ref,
                       qah_ref, qih_ref, key_ref, m_ref, l_ref, acc_ref):
    t = qa_ref.shape[0]
    i = pl.program_id(1)
    nj = i + 1
    row = lax.broadcasted_iota(I32, (t, t), 0)
    col = lax.broadcasted_iota(I32, (t, t), 1)
    lo = lax.broadcasted_iota(I32, (t, LANES), 1) < HEAD_DIM

    _split_halves(qa_ref, qah_ref, N_HEADS_A // 2)
    _split_halves(qi_ref, qih_ref, N_IDX_HEADS // 2)

    def score_chunk(j, carry):
        k0 = pl.multiple_of(j * t, t)
        kc = k2_ref[pl.ds(k0, t), :]
        wi = wi_ref[...]
        acc = jnp.zeros((t, t), F32)
        for h in range(N_IDX_HEADS):
            acc = acc + jnp.maximum(_dot_nt(qih_ref[h], kc), 0.0) * wi[:, h:h + 1]
        key = _sortable_key(acc)
        key_ref[j] = jnp.where(k0 + col <= i * t + row, key, jnp.int32(INT_MIN))
        return carry

    lax.fori_loop(0, nj, score_chunk, 0)

    def count_where(pred_fn):
        def body(j, c):
            return c + _fold_lanes(jnp.where(pred_fn(key_ref[j], j), 1.0, 0.0))
        c = lax.fori_loop(0, nj, body, jnp.zeros((t, LANES), F32))
        return jnp.sum(c, axis=1, keepdims=True)

    def bit_step(b, tu):
        cand_u = tu | lax.shift_left(jnp.int32(1), 31 - b)
        cand = cand_u ^ jnp.int32(INT_MIN)
        cnt = count_where(lambda kc, j: kc >= cand)
        return jnp.where(cnt >= k_sel, cand_u, tu)

    tu = lax.fori_loop(0, 32, bit_step, jnp.zeros((t, 1), I32))
    thr = tu ^ jnp.int32(INT_MIN)

    n_gt = count_where(lambda kc, j: kc > thr)
    n_eq = count_where(lambda kc, j: kc == thr)
    need = k_sel - n_gt
    tied = jnp.where((n_eq > need) & (thr != jnp.int32(INT_MIN)), 1.0, 0.0)

    def tie_search(_):
        def cut_step(b, c0):
            cand = c0 | lax.shift_left(jnp.int32(1), n_bits - 1 - b)
            cnt = count_where(lambda kc, j: (kc == thr) & (j * t + col < cand))
            return jnp.where(cnt < need, cand, c0)
        return lax.fori_loop(0, n_bits, cut_step, jnp.zeros((t, 1), I32))

    cut = lax.cond(jnp.max(tied) > 0.0, tie_search,
                   lambda _: jnp.full((t, 1), 2 ** 30, I32), 0)

    m_ref[...] = jnp.full(m_ref.shape, NEG, F32)
    l_ref[...] = jnp.zeros(l_ref.shape, F32)
    acc_ref[...] = jnp.zeros(acc_ref.shape, F32)

    def attn_chunk(j, carry):
        k0 = pl.multiple_of(j * t, t)
        kc = key_ref[j]
        sel = (kc > thr) | ((kc == thr) & (k0 + col <= cut))
        dsel = jnp.minimum(i - j, 2)
        for p in range(N_HEADS_A // 2):
            kp = kv_ref[pl.ds(k0, t), p * LANES:(p + 1) * LANES]
            vp = kv_ref[pl.ds(k0, t), A_WIDTH + p * LANES:A_WIDTH + (p + 1) * LANES]
            pv, al = [], []
            for half in range(2):
                h = 2 * p + half
                s = jnp.where(sel, _dot_nt(qah_ref[h], kp) + bt_ref[dsel, h], NEG)
                pr, alpha = _online_softmax_step(s, h, m_ref, l_ref)
                pv.append(_dot(pr.astype(BF16), vp))
                al.append(alpha)
            acc_ref[p] = acc_ref[p] * jnp.where(lo, al[0], al[1]) + jnp.where(lo, pv[0], pv[1])
        return carry

    lax.fori_loop(0, nj, attn_chunk, 0)

    for p in range(N_HEADS_A // 2):
        linv = jnp.where(lo, 1.0 / l_ref[2 * p], 1.0 / l_ref[2 * p + 1])
        o_ref[:, p * LANES:(p + 1) * LANES] = (acc_ref[p] * linv).astype(BF16)


def _dsa_prompt(qa, qi, wi, ki16, kva16, bias_tiles, batch, seq):
    t = min(ATTN_TILE, seq)
    nq = seq // t
    k_sel = min(TOPK_MAX, seq // 4)
    n_bits = max(1, (seq - 1).bit_length())
    kern = functools.partial(_dsa_prompt_kernel, k_sel, n_bits)
    return pl.pallas_call(
        kern,
        grid=(batch, nq),
        in_specs=[pl.BlockSpec((t, A_WIDTH), lambda b, i: (b * nq + i, 0)),
                  pl.BlockSpec((t, A_WIDTH), lambda b, i: (b * nq + i, 0)),
                  pl.BlockSpec((t, LANES), lambda b, i: (b * nq + i, 0)),
                  pl.BlockSpec((seq, LANES), lambda b, i: (b, 0)),
                  pl.BlockSpec((seq, 2 * A_WIDTH), lambda b, i: (b, 0)),
                  pl.BlockSpec(bias_tiles.shape, lambda b, i: (0, 0, 0, 0))],
        out_specs=pl.BlockSpec((t, A_WIDTH), lambda b, i: (b * nq + i, 0)),
        out_shape=jax.ShapeDtypeStruct((batch * seq, A_WIDTH), BF16),
        scratch_shapes=[pltpu.VMEM((N_HEADS_A, t, LANES), BF16),
                        pltpu.VMEM((N_IDX_HEADS, t, LANES), BF16),
                        pltpu.VMEM((nq, t, t), I32),
                        pltpu.VMEM((N_HEADS_A, t, LANES), F32),
                        pltpu.VMEM((N_HEADS_A, t, LANES), F32),
                        pltpu.VMEM((N_HEADS_A // 2, t, LANES), F32)],
        compiler_params=_cparams(("arbitrary", "arbitrary")),
        name="dsa_prompt",
    )(qa, qi, wi, ki16, kva16, bias_tiles)


def _subln(o0, o1, lam, lam_init, g):
    d = o0 - lam * o1
    return _rms(d, g) * (1.0 - lam_init)


def _diff_prompt_kernel(lam_init, lam_ref, qb_ref, kb_ref, vb_ref, bt_ref, g_ref, o_ref,
                        qh_ref, m_ref, l_ref, acc_ref):
    t = qb_ref.shape[0]
    i = pl.program_id(1)
    n_maps = 2 * N_HEADS_B
    _split_halves(qb_ref, qh_ref, N_HEADS_B)
    m_ref[...] = jnp.full(m_ref.shape, NEG, F32)
    l_ref[...] = jnp.zeros(l_ref.shape, F32)
    acc_ref[...] = jnp.zeros(acc_ref.shape, F32)

    def chunk(j, carry):
        k0 = pl.multiple_of(j * t, t)
        dsel = jnp.minimum(i - j, 2)
        for h in range(N_HEADS_B):
            kp = kb_ref[pl.ds(k0, t), h * LANES:(h + 1) * LANES]
            vp = vb_ref[pl.ds(k0, t), h * LANES:(h + 1) * LANES]
            for mp in range(2):
                c = 2 * h + mp
                s = _dot_nt(qh_ref[c], kp) + bt_ref[dsel, c]
                pr, alpha = _online_softmax_step(s, c, m_ref, l_ref)
                acc_ref[c] = acc_ref[c] * alpha + _dot(pr.astype(BF16), vp)
        return carry

    lax.fori_loop(0, i + 1, chunk, 0)

    lam = lam_ref[0]
    for h in range(N_HEADS_B):
        o0 = acc_ref[2 * h] / l_ref[2 * h]
        o1 = acc_ref[2 * h + 1] / l_ref[2 * h + 1]
        o_ref[:, h * LANES:(h + 1) * LANES] = _subln(o0, o1, lam, lam_init, g_ref[...]).astype(BF16)
    del n_maps


def _diff_prompt(lam, lam_init, qb, kb16, vb16, bias_tiles, g_sub, batch, seq):
    t = min(ATTN_TILE, seq)
    nq = seq // t
    n_maps = 2 * N_HEADS_B
    kern = functools.partial(_diff_prompt_kernel, lam_init)
    return pl.pallas_call(
        kern,
        grid=(batch, nq),
        in_specs=[pl.BlockSpec(memory_space=pltpu.SMEM),
                  pl.BlockSpec((t, B_WIDTH), lambda b, i: (b * nq + i, 0)),
                  pl.BlockSpec((seq, B_WIDTH), lambda b, i: (b, 0)),
                  pl.BlockSpec((seq, B_WIDTH), lambda b, i: (b, 0)),
                  pl.BlockSpec(bias_tiles.shape, lambda b, i: (0, 0, 0, 0)),
                  pl.BlockSpec((1, LANES), lambda b, i: (0, 0))],
        out_specs=pl.BlockSpec((t, B_WIDTH), lambda b, i: (b * nq + i, 0)),
        out_shape=jax.ShapeDtypeStruct((batch * seq, B_WIDTH), BF16),
        scratch_shapes=[pltpu.VMEM((n_maps, t, LANES), BF16),
                        pltpu.VMEM((n_maps, t, LANES), F32),
                        pltpu.VMEM((n_maps, t, LANES), F32),
                        pltpu.VMEM((n_maps, t, LANES), F32)],
        compiler_params=_cparams(("arbitrary", "arbitrary")),
        name="diff_prompt",
    )(lam, qb, kb16, vb16, bias_tiles, g_sub)


SAMPLE_ROWS = 8


def _sample_index_kernel(k_sel, n_pages, g_pages, dec_seq, n_bits, pt_ref, *refs):
    page_refs = refs[:g_pages]
    qx_ref, wx_ref, knew_ref, mask_ref, key_ref = refs[g_pages:]
    del pt_ref
    step = pl.program_id(1)
    n_steps = n_pages // g_pages
    r8 = SAMPLE_ROWS

    def chunk_scores(kc16):
        d = jnp.maximum(_dot_nt(qx_ref[0], kc16), 0.0) * wx_ref[0]
        acc = d[0:r8]
        for h in range(1, N_IDX_HEADS):
            acc = acc + d[h * r8:(h + 1) * r8]
        return _sortable_key(acc)

    for g in range(g_pages):
        key_ref[step * g_pages + g] = chunk_scores(page_refs[g][0].astype(BF16))

    @pl.when(step == n_steps - 1)
    def _():
        row = lax.broadcasted_iota(I32, (r8, LANES), 0)
        lane = lax.broadcasted_iota(I32, (r8, LANES), 1)
        key_new = chunk_scores(knew_ref[0])
        key_ref[n_pages] = jnp.where((lane <= row) & (lane < dec_seq), key_new, jnp.int32(INT_MIN))

        n_chunks = n_pages + 1
        shape3 = (n_chunks, r8, LANES)
        colg = lax.broadcasted_iota(I32, shape3, 0) * LANES + lax.broadcasted_iota(I32, shape3, 2)

        def count(pred):
            c = jnp.sum(jnp.where(pred, 1.0, 0.0), axis=0)
            return jnp.sum(c, axis=1, keepdims=True)[None]

        def bit_step(b, tu):
            cand_u = tu | lax.shift_left(jnp.int32(1), 31 - b)
            cand = cand_u ^ jnp.int32(INT_MIN)
            return jnp.where(count(key_ref[...] >= cand) >= k_sel, cand_u, tu)

        tu = lax.fori_loop(0, 32, bit_step, jnp.zeros((1, r8, 1), I32))
        thr = tu ^ jnp.int32(INT_MIN)
        keys = key_ref[...]
        need = k_sel - count(keys > thr)

        def cut_step(b, c0):
            cand = c0 | lax.shift_left(jnp.int32(1), n_bits - 1 - b)
            return jnp.where(count((key_ref[...] == thr) & (colg < cand)) < need, cand, c0)

        cut = lax.fori_loop(0, n_bits, cut_step, jnp.zeros((1, r8, 1), I32))
        sel = (keys > thr) | ((keys == thr) & (colg <= cut))
        sel = sel & (keys != jnp.int32(INT_MIN))
        mask_ref[0] = jnp.where(sel, 0.0, NEG)


def _sample_index(page_table, cache_k_idx, qx, wx, knew, dec_seq, g_pages):
    batch, n_pages = page_table.shape
    past = n_pages * LANES
    k_sel = min(TOPK_MAX, (past + dec_seq) // 4)
    n_bits = max(1, (past + LANES - 1).bit_length())
    n_steps = n_pages // g_pages
    rows = N_IDX_HEADS * SAMPLE_ROWS
    kern = functools.partial(_sample_index_kernel, k_sel, n_pages, g_pages, dec_seq, n_bits)
    page_specs = [pl.BlockSpec((1, LANES, D_IDX), functools.partial(
        lambda b, s, pt, g: (pt[b, s * g_pages + g], 0, 0), g=g)) for g in range(g_pages)]
    grid_spec = pltpu.PrefetchScalarGridSpec(
        num_scalar_prefetch=1,
        grid=(batch, n_steps),
        in_specs=page_specs + [pl.BlockSpec((1, rows, D_IDX), lambda b, s, pt: (b, 0, 0)),
                               pl.BlockSpec((1, rows, LANES), lambda b, s, pt: (b, 0, 0)),
                               pl.BlockSpec((1, LANES, D_IDX), lambda b, s, pt: (b, 0, 0))],
        out_specs=pl.BlockSpec((1, n_pages + 1, SAMPLE_ROWS, LANES), lambda b, s, pt: (b, 0, 0, 0)),
        scratch_shapes=[pltpu.VMEM((n_pages + 1, SAMPLE_ROWS, LANES), I32)])
    return pl.pallas_call(
        kern,
        grid_spec=grid_spec,
        out_shape=jax.ShapeDtypeStruct((batch, n_pages + 1, SAMPLE_ROWS, LANES), F32),
        compiler_params=_cparams(("arbitrary", "arbitrary")),
        name="sample_index",
    )(page_table, *([cache_k_idx] * g_pages), qx, wx, knew)


def _sample_attn_kernel(lam_init, n_pages, g_pages, pt_ref, lam_ref, *refs):
    kva_refs = refs[:g_pages]
    kb_refs = refs[g_pages:2 * g_pages]
    vb_refs = refs[2 * g_pages:3 * g_pages]
    (qa_ref, qb_ref, mask_ref, maskn_ref, ba_ref, bb_ref, kvan_ref, kbn_ref, vbn_ref, g_ref,
     o_ref, m_ref, l_ref, acca_ref, accb_ref) = refs[3 * g_pages:]
    del pt_ref
    step = pl.program_id(1)
    n_steps = n_pages // g_pages
    r8 = SAMPLE_ROWS
    rows = N_HEADS_A * r8

    @pl.when(step == 0)
    def _():
        m_ref[...] = jnp.full(m_ref.shape, NEG, F32)
        l_ref[...] = jnp.zeros(l_ref.shape, F32)
        acca_ref[...] = jnp.zeros(acca_ref.shape, F32)
        accb_ref[...] = jnp.zeros(accb_ref.shape, F32)

    def update(mixer, q_ref, k16, v16, extra, acc_ref):
        s = _dot_nt(q_ref[0], k16) + extra
        pr, alpha = _online_softmax_step(s, mixer, m_ref, l_ref)
        acc_ref[...] = acc_ref[...] * jnp.concatenate([alpha] * (A_WIDTH // LANES), axis=1) \
            + _dot(pr.astype(BF16), v16)

    def both(kva16, kb16, vb16, mask8, bsel):
        mask = jnp.concatenate([mask8] * N_HEADS_A, axis=0)
        update(0, qa_ref, kva16[:, :A_WIDTH], kva16[:, A_WIDTH:], ba_ref[bsel] + mask, acca_ref)
        update(1, qb_ref, kb16, vb16, bb_ref[bsel], accb_ref)

    for g in range(g_pages):
        page = step * g_pages + g
        bsel = jnp.where(page == n_pages - 1, 1, 0)
        both(kva_refs[g][0].astype(BF16), kb_refs[g][0].astype(BF16), vb_refs[g][0].astype(BF16),
             mask_ref[0, g], bsel)

    @pl.when(step == n_steps - 1)
    def _():
        both(kvan_ref[0], kbn_ref[0], vbn_ref[0], maskn_ref[0, 0], 2)
        lane = lax.broadcasted_iota(I32, (r8, A_WIDTH), 1)
        oa = jnp.zeros((r8, A_WIDTH), F32)
        la = jnp.concatenate([l_ref[0]] * (A_WIDTH // LANES), axis=1)
        acca = acca_ref[...] / la
        for h in range(N_HEADS_A):
            blk = acca[h * r8:(h + 1) * r8]
            oa = jnp.where((lane >= h * HEAD_DIM) & (lane < (h + 1) * HEAD_DIM), blk, oa)
        o_ref[0, :, :A_WIDTH] = oa.astype(BF16)
        lam = lam_ref[0]
        lb = l_ref[1]
        for h in range(N_HEADS_B):
            r0 = 2 * h * r8
            o0 = accb_ref[r0:r0 + r8, h * LANES:(h + 1) * LANES] / lb[r0:r0 + r8]
            o1 = accb_ref[r0 + r8:r0 + 2 * r8, h * LANES:(h + 1) * LANES] / lb[r0 + r8:r0 + 2 * r8]
            o_ref[0, :, A_WIDTH + h * LANES:A_WIDTH + (h + 1) * LANES] = _subln(
                o0, o1, lam, lam_init, g_ref[...]).astype(BF16)
    del rows


def _sample_attn(lam, lam_init, page_table, cache_kva, cache_kb, cache_vb, qa_x, qb_x, mask, bias_a, bias_b,
                 kva_new, kb_new, vb_new, g_sub, g_pages):
    batch, n_pages = page_table.shape
    n_steps = n_pages // g_pages
    rows = N_HEADS_A * SAMPLE_ROWS
    kern = functools.partial(_sample_attn_kernel, lam_init, n_pages, g_pages)

    def page_spec(width, g):
        return pl.BlockSpec((1, LANES, width), lambda b, s, pt: (pt[b, s * g_pages + g], 0, 0))

    in_specs = ([pl.BlockSpec(memory_space=pltpu.SMEM)]
                + [page_spec(2 * A_WIDTH, g) for g in range(g_pages)]
                + [page_spec(B_WIDTH, g) for g in range(g_pages)]
                + [page_spec(B_WIDTH, g) for g in range(g_pages)]
                + [pl.BlockSpec((1, rows, A_WIDTH), lambda b, s, pt: (b, 0, 0)),
                   pl.BlockSpec((1, rows, B_WIDTH), lambda b, s, pt: (b, 0, 0)),
                   pl.BlockSpec((1, g_pages, SAMPLE_ROWS, LANES), lambda b, s, pt: (b, s, 0, 0)),
                   pl.BlockSpec((1, 1, SAMPLE_ROWS, LANES), lambda b, s, pt: (b, n_pages, 0, 0)),
                   pl.BlockSpec(bias_a.shape, lambda b, s, pt: (0, 0, 0)),
                   pl.BlockSpec(bias_b.shape, lambda b, s, pt: (0, 0, 0)),
                   pl.BlockSpec((1, LANES, 2 * A_WIDTH), lambda b, s, pt: (b, 0, 0)),
                   pl.BlockSpec((1, LANES, B_WIDTH), lambda b, s, pt: (b, 0, 0)),
                   pl.BlockSpec((1, LANES, B_WIDTH), lambda b, s, pt: (b, 0, 0)),
                   pl.BlockSpec((1, LANES), lambda b, s, pt: (0, 0))])
    grid_spec = pltpu.PrefetchScalarGridSpec(
        num_scalar_prefetch=1,
        grid=(batch, n_steps),
        in_specs=in_specs,
        out_specs=pl.BlockSpec((1, SAMPLE_ROWS, A_WIDTH + B_WIDTH), lambda b, s, pt: (b, 0, 0)),
        scratch_shapes=[pltpu.VMEM((2, rows, LANES), F32),
                        pltpu.VMEM((2, rows, LANES), F32),
                        pltpu.VMEM((rows, A_WIDTH), F32),
                        pltpu.VMEM((rows, B_WIDTH), F32)])
    return pl.pallas_call(
        kern,
        grid_spec=grid_spec,
        out_shape=jax.ShapeDtypeStruct((batch, SAMPLE_ROWS, A_WIDTH + B_WIDTH), BF16),
        compiler_params=_cparams(("arbitrary", "arbitrary")),
        name="sample_attn",
    )(page_table, lam, *([cache_kva] * g_pages), *([cache_kb] * g_pages), *([cache_vb] * g_pages),
      qa_x, qb_x, mask, mask, bias_a, bias_b, kva_new, kb_new, vb_new, g_sub)


def _sample_bias(table, past, dec_seq, causal_new):
    t = jnp.arange(SAMPLE_ROWS, dtype=I32)[:, None]
    c = jnp.arange(LANES, dtype=I32)[None, :]
    far = jnp.full((SAMPLE_ROWS, LANES), MAX_DISTANCE, I32)
    last = t + LANES - c
    new = t - c
    out = []
    for dist in (far, last, new):
        b = jnp.transpose(table[_t5_bucket(dist)].astype(F32), (2, 0, 1))
        out.append(b.reshape(table.shape[1] * SAMPLE_ROWS, LANES))
    if causal_new:
        ok = jnp.tile((new >= 0) & (c < dec_seq), (table.shape[1], 1))
        out[2] = jnp.where(ok, out[2], NEG)
    del past
    return jnp.stack(out)


def _block_diag_rows(q):
    b, t = q.shape[:2]
    q = jnp.pad(q, ((0, 0), (0, SAMPLE_ROWS - t), (0, 0), (0, 0)))
    q = jnp.transpose(q, (0, 2, 1, 3))
    eye = jnp.eye(8, dtype=q.dtype)
    return jnp.einsum('bctd,cg->bctgd', q, eye).reshape(b, 8 * SAMPLE_ROWS, 8 * HEAD_DIM)


def _tail1_kernel(n_mix, x_ref, *refs):
    o_refs = refs[:n_mix]
    w_refs = refs[n_mix:2 * n_mix]
    g_ref, wq_ref, x1_ref, qm_ref = refs[2 * n_mix:]
    x1 = x_ref[...]
    for o_ref, w_ref in zip(o_refs, w_refs):
        x1 = x1 + _dot(o_ref[...], w_ref[...])
    x1_ref[...] = x1
    hm = _rms(x1, g_ref[...]).astype(BF16)
    qm_ref[...] = (_dot(hm, wq_ref[...]) * HEAD_DIM ** -0.5).astype(BF16)


def _tail1(x, mixes, g_mem, wq16):
    n, d = x.shape
    tm = min(n, 512)
    n_mix = len(mixes)
    mw = wq16.shape[1]
    return pl.pallas_call(
        functools.partial(_tail1_kernel, n_mix),
        grid=(n // tm,),
        in_specs=([pl.BlockSpec((tm, d), lambda i: (i, 0))]
                  + [pl.BlockSpec((tm, o.shape[1]), lambda i: (i, 0)) for o, _ in mixes]
                  + [pl.BlockSpec(w.shape, lambda i: (0, 0)) for _, w in mixes]
                  + [pl.BlockSpec((1, d), lambda i: (0, 0)),
                     pl.BlockSpec(wq16.shape, lambda i: (0, 0))]),
        out_specs=[pl.BlockSpec((tm, d), lambda i: (i, 0)), pl.BlockSpec((tm, mw), lambda i: (i, 0))],
        out_shape=[jax.ShapeDtypeStruct((n, d), F32), jax.ShapeDtypeStruct((n, mw), BF16)],
        compiler_params=_cparams(("arbitrary",)),
        name="tail1",
    )(x, *[o for o, _ in mixes], *[w for _, w in mixes], g_mem, wq16)


def _tail2_kernel(qm_ref, mkv_ref, x1_ref, wo_ref, g_ref, wr_ref, br_ref, x2_ref, h_ref, route_ref):
    tq = qm_ref.shape[1]
    mem_w = N_HEADS_MEM * HEAD_DIM
    lo = lax.broadcasted_iota(I32, (tq, LANES), 1) < HEAD_DIM
    mkv = mkv_ref[0].astype(BF16)
    outs = []
    for p in range(N_HEADS_MEM // 2):
        qp = qm_ref[0, :, p * LANES:(p + 1) * LANES]
        kp = mkv[:, p * LANES:(p + 1) * LANES]
        vp = mkv[:, mem_w + p * LANES:mem_w + (p + 1) * LANES]
        pv = []
        for half in range(2):
            qh = jnp.where(lo == (half == 0), qp, jnp.zeros_like(qp))
            s = _dot_nt(qh, kp)
            e = jnp.exp(s - jnp.max(s, axis=1, keepdims=True))
            pv.append(_dot(e.astype(BF16), vp) / jnp.sum(e, axis=1, keepdims=True))
        outs.append(jnp.where(lo, pv[0], pv[1]))
    om = jnp.concatenate(outs, axis=1).astype(BF16)
    x2 = x1_ref[0] + _dot(om, wo_ref[...])
    x2_ref[0] = x2
    h = _rms(x2, g_ref[...])
    h_ref[0] = h

    lg = jnp.dot(h, wr_ref[...], preferred_element_type=F32, precision=lax.Precision.HIGHEST) + br_ref[...]
    lane = lax.broadcasted_iota(I32, (tq, LANES), 1)
    lane_f = lane.astype(F32)
    big = jnp.float32(1e9)
    is_g = lane < N_GROUPS
    m1 = jnp.max(jnp.where(is_g, lg, -jnp.inf), axis=1, keepdims=True)
    grp = jnp.min(jnp.where(is_g & (lg == m1), lane_f, big), axis=1, keepdims=True)
    p_grp = 1.0 / jnp.sum(jnp.where(is_g, jnp.exp(lg - m1), 0.0), axis=1, keepdims=True)
    e0 = N_GROUPS + grp * EXPERTS_PER_GROUP
    in_g = (lane_f >= e0) & (lane_f < e0 + EXPERTS_PER_GROUP)
    v1 = jnp.max(jnp.where(in_g, lg, -jnp.inf), axis=1, keepdims=True)
    i1 = jnp.min(jnp.where(in_g & (lg == v1), lane_f, big), axis=1, keepdims=True)
    rest = in_g & (lane_f != i1)
    v2 = jnp.max(jnp.where(rest, lg, -jnp.inf), axis=1, keepdims=True)
    i2 = jnp.min(jnp.where(rest & (lg == v2), lane_f, big), axis=1, keepdims=True)
    e2 = jnp.exp(v2 - v1)
    den = 1.0 + e2
    g1 = (1.0 / den) * p_grp
    g2 = (e2 / den) * p_grp
    route = jnp.where(lane == 0, i1 - N_GROUPS, 0.0)
    route = jnp.where(lane == 1, i2 - N_GROUPS, route)
    route = jnp.where(lane == 2, g1, route)
    route = jnp.where(lane == 3, g2, route)
    route_ref[0] = route


def _tail2(qm, mkv, x1, wo16, g_ffn, wr, br):
    b, t, d = x1.shape
    tq = min(t, 512)
    mem_w = qm.shape[2]
    n_mem = mkv.shape[1]
    return pl.pallas_call(
        _tail2_kernel,
        grid=(b, t // tq),
        in_specs=[pl.BlockSpec((1, tq, mem_w), lambda i, j: (i, j, 0)),
                  pl.BlockSpec((1, n_mem, 2 * mem_w), lambda i, j: (i, 0, 0)),
                  pl.BlockSpec((1, tq, d), lambda i, j: (i, j, 0)),
                  pl.BlockSpec(wo16.shape, lambda i, j: (0, 0)),
                  pl.BlockSpec((1, d), lambda i, j: (0, 0)),
                  pl.BlockSpec(wr.shape, lambda i, j: (0, 0)),
                  pl.BlockSpec((1, LANES), lambda i, j: (0, 0))],
        out_specs=[pl.BlockSpec((1, tq, d), lambda i, j: (i, j, 0)),
                   pl.BlockSpec((1, tq, d), lambda i, j: (i, j, 0)),
                   pl.BlockSpec((1, tq, LANES), lambda i, j: (i, j, 0))],
        out_shape=[jax.ShapeDtypeStruct((b, t, d), F32), jax.ShapeDtypeStruct((b, t, d), F32),
                   jax.ShapeDtypeStruct((b, t, LANES), F32)],
        compiler_params=_cparams(("arbitrary", "arbitrary")),
        name="tail2",
    )(qm, mkv, x1, wo16, g_ffn, wr, br)


def _gather_rows_kernel(n_rows, idx_ref, src_ref, dst_ref, sem):
    def copy(r):
        return pltpu.make_async_copy(src_ref.at[pl.ds(idx_ref[r], 1)], dst_ref.at[pl.ds(r, 1)], sem)

    def body(r, carry):
        copy(r).start()

        @pl.when(r >= GATHER_WINDOW)
        def _():
            copy(r - GATHER_WINDOW).wait()
        return carry

    lax.fori_loop(0, n_rows, body, 0)

    def drain(r, carry):
        copy(r).wait()
        return carry

    lax.fori_loop(max(0, n_rows - GATHER_WINDOW), n_rows, drain, 0)


def _gather_rows(src, idx):
    n_rows = idx.shape[0]
    grid_spec = pltpu.PrefetchScalarGridSpec(
        num_scalar_prefetch=1,
        grid=(1,),
        in_specs=[pl.BlockSpec(memory_space=pl.ANY)],
        out_specs=pl.BlockSpec(memory_space=pl.ANY),
        scratch_shapes=[pltpu.SemaphoreType.DMA(())])
    return pl.pallas_call(
        functools.partial(_gather_rows_kernel, n_rows),
        grid_spec=grid_spec,
        out_shape=jax.ShapeDtypeStruct((n_rows, src.shape[1]), src.dtype),
        compiler_params=pltpu.CompilerParams(dimension_semantics=("arbitrary",), has_side_effects=True),
        name="gather_rows",
    )(idx, src)


def _experts_kernel(be_ref, x_ref, wgu_ref, wdn_ref, y_ref):
    del be_ref
    gu = _dot(x_ref[...].astype(BF16), wgu_ref[0])
    gate = gu[:, :D_EXPERT]
    up = gu[:, D_EXPERT:]
    act = gate * (1.0 / (1.0 + jnp.exp(-gate))) * up
    y_ref[...] = _dot(act.astype(BF16), wdn_ref[0])


def _experts(block_expert, x_rows, wgu16, wdn16):
    n_rows, d = x_rows.shape
    n_blocks = n_rows // EXPERT_ROWS
    grid_spec = pltpu.PrefetchScalarGridSpec(
        num_scalar_prefetch=1,
        grid=(n_blocks,),
        in_specs=[pl.BlockSpec((EXPERT_ROWS, d), lambda i, be: (i, 0)),
                  pl.BlockSpec((1,) + wgu16.shape[1:], lambda i, be: (be[i], 0, 0)),
                  pl.BlockSpec((1,) + wdn16.shape[1:], lambda i, be: (be[i], 0, 0))],
        out_specs=pl.BlockSpec((EXPERT_ROWS, d), lambda i, be: (i, 0)))
    return pl.pallas_call(
        _experts_kernel,
        grid_spec=grid_spec,
        out_shape=jax.ShapeDtypeStruct((n_rows, d), F32),
        compiler_params=_cparams(("arbitrary",)),
        name="experts",
    )(block_expert, x_rows, wgu16, wdn16)


def _final_kernel(x2_ref, y_ref, route_ref, g_ref, o_ref):
    d = x2_ref.shape[1]
    route = route_ref[...]
    moe = y_ref[:, :d] * route[:, 2:3] + y_ref[:, d:] * route[:, 3:4]
    o_ref[...] = _rms(x2_ref[...] + moe, g_ref[...])


def _final(x2, y_pairs, route, g_final):
    n, d = x2.shape
    tm = min(n, 512)
    return pl.pallas_call(
        _final_kernel,
        grid=(n // tm,),
        in_specs=[pl.BlockSpec((tm, d), lambda i: (i, 0)),
                  pl.BlockSpec((tm, 2 * d), lambda i: (i, 0)),
                  pl.BlockSpec((tm, LANES), lambda i: (i, 0)),
                  pl.BlockSpec((1, d), lambda i: (0, 0))],
        out_specs=pl.BlockSpec((tm, d), lambda i: (i, 0)),
        out_shape=jax.ShapeDtypeStruct((n, d), F32),
        compiler_params=_cparams(("arbitrary",)),
        name="final",
    )(x2, y_pairs, route, g_final)


def _moe_and_final(x2, h_ffn, route, wgu16, wdn16, g_final):
    n, d = x2.shape
    n_assign = 2 * n
    experts = route[:, :2].astype(I32).reshape(n_assign)
    onehot = (experts[:, None] == jnp.arange(N_EXPERTS, dtype=I32)[None, :]).astype(I32)
    running = jnp.cumsum(onehot, axis=0)
    rank = jnp.sum((running - onehot) * onehot, axis=1)
    counts = running[-1]
    padded = (counts + EXPERT_ROWS - 1) // EXPERT_ROWS * EXPERT_ROWS
    pad_ends = jnp.cumsum(padded)
    pad_starts = pad_ends - padded
    pos = jnp.sum(onehot * pad_starts[None, :], axis=1) + rank
    n_blocks = -(-n_assign // EXPERT_ROWS) + N_EXPERTS
    n_rows = n_blocks * EXPERT_ROWS
    tok = jnp.arange(n_assign, dtype=I32) // 2
    row_src = jnp.zeros((n_rows,), I32).at[pos].set(tok)
    block_expert = jnp.minimum(
        jnp.searchsorted(pad_ends, jnp.arange(n_blocks, dtype=I32) * EXPERT_ROWS, side='right'),
        N_EXPERTS - 1).astype(I32)
    x_rows = _gather_rows(h_ffn, row_src)
    y_rows = _experts(block_expert, x_rows, wgu16, wdn16)
    y_pairs = _gather_rows(y_rows, pos.astype(I32)).reshape(n, 2 * d)
    return _final(x2, y_pairs, route, g_final)


def kernel(x_prompt, x_sample, cache_kv_a, cache_k_idx, cache_k_b, cache_v_b, cache_mem_kv, page_table, mem_prompt, rel_bias, g_mix, w_in, g_kidx, lambda_q1, lambda_k1, lambda_q2, lambda_k2, g_subln, w_out, g_mem, g_memin, w_mem_q, w_mem_kv, w_mem_o, g_ffn, w_router1, b_router1, w_router2, b_router2, w_gate_up, w_down, g_final):
    depth = g_mix.shape[0]
    assert depth == 1
    layer = 0
    batch, seq, d = x_prompt.shape
    dec_batch, dec_seq, _ = x_sample.shape
    n_pages = page_table.shape[1]
    n_phys = cache_kv_a.shape[1]
    past = n_pages * LANES
    n_mem = mem_prompt.shape[1]
    mem_w = N_HEADS_MEM * HEAD_DIM
    assert cache_kv_a.shape[2] == LANES and dec_seq <= SAMPLE_ROWS

    table_a = rel_bias[:, :N_HEADS_A]
    table_b = rel_bias[:, N_HEADS_A:]
    lam_init = 0.8 - 0.6 * math.exp(-0.3 * layer)
    lam = (jnp.exp(jnp.sum(lambda_q1[layer] * lambda_k1[layer]).astype(F32))
           - jnp.exp(jnp.sum(lambda_q2[layer] * lambda_k2[layer]).astype(F32)) + lam_init).reshape(1)

    w = w_in[layer]
    offs = [0, 512, 1024, 1536, 2048, 2112, 2120, 2632, 3144, 3656]
    q_a, k_a, v_a, q_i, k_i, w_i, q_b, k_b, v_b = [w[:, offs[s]:offs[s + 1]] for s in range(9)]
    w_packed = jnp.concatenate(
        [q_a, k_a, v_a, q_i, k_i, k_i, jnp.pad(w_i, ((0, 0), (0, LANES - N_IDX_HEADS))), q_b, k_b, v_b],
        axis=1).astype(BF16)
    gk2 = jnp.concatenate([g_kidx[layer], g_kidx[layer]]).reshape(1, LANES)
    g_mix_l = g_mix[layer].reshape(1, d)
    w_out16 = w_out[layer].astype(BF16)
    wq16 = w_mem_q[layer].astype(BF16)
    wo16 = w_mem_o[layer].astype(BF16)
    wr = jnp.concatenate([w_router1[layer],
                          jnp.transpose(w_router2[layer], (1, 0, 2)).reshape(d, N_EXPERTS),
                          jnp.zeros((d, LANES - N_GROUPS - N_EXPERTS), F32)], axis=1)
    br = jnp.concatenate([b_router1[layer], b_router2[layer].reshape(N_EXPERTS),
                          jnp.zeros((LANES - N_GROUPS - N_EXPERTS,), F32)]).reshape(1, LANES)
    wgu16 = w_gate_up[layer].astype(BF16)
    wdn16 = w_down[layer].astype(BF16)
    g_sub = g_subln[layer].reshape(1, LANES)
    g_mem_l = g_mem[layer].reshape(1, d)
    g_ffn_l = g_ffn[layer].reshape(1, d)
    g_fin = g_final.reshape(1, d)

    n_p = batch * seq
    (qa, kva, kva16, qi, ki, ki16, wi, qb, kb, vb, kb16, vb16) = _proj(
        x_prompt.reshape(n_p, d), g_mix_l, w_packed, gk2)
    t_attn = min(ATTN_TILE, seq)
    o_a = _dsa_prompt(qa, qi, wi, ki16, kva16, _prompt_bias_tiles(table_a, t_attn), batch, seq)
    o_b = _diff_prompt(lam, lam_init, qb, kb16, vb16, _prompt_bias_tiles(table_b, t_attn), g_sub, batch, seq)
    mkv_p = _norm_matmul(mem_prompt.reshape(batch * n_mem, d), g_memin[layer].reshape(1, d),
                         w_mem_kv[layer].astype(BF16))
    x1, qm = _tail1(x_prompt.reshape(n_p, d), [(o_a, w_out16[:A_WIDTH]), (o_b, w_out16[A_WIDTH:])],
                    g_mem_l, wq16)
    x2, h_ffn, route = _tail2(qm.reshape(batch, seq, mem_w), mkv_p.reshape(batch, n_mem, 2 * mem_w),
                              x1.reshape(batch, seq, d), wo16, g_ffn_l, wr, br)
    y_prompt = _moe_and_final(x2.reshape(n_p, d), h_ffn.reshape(n_p, d), route.reshape(n_p, LANES),
                              wgu16, wdn16, g_fin).reshape(batch, seq, d)

    n_s = dec_batch * dec_seq
    (qa_s, kva_s, kva16_s, qi_s, ki_s, ki16_s, wi_s, qb_s, kb_s, vb_s, kb16_s, vb16_s) = _proj(
        x_sample.reshape(n_s, d), g_mix_l, w_packed, gk2)
    pad_t = SAMPLE_ROWS - dec_seq
    qx = jnp.transpose(jnp.pad(qi_s.reshape(dec_batch, dec_seq, N_IDX_HEADS, D_IDX),
                               ((0, 0), (0, pad_t), (0, 0), (0, 0))), (0, 2, 1, 3)
                       ).reshape(dec_batch, N_IDX_HEADS * SAMPLE_ROWS, D_IDX)
    wx = jnp.transpose(jnp.pad(wi_s[:, :N_IDX_HEADS].reshape(dec_batch, dec_seq, N_IDX_HEADS),
                               ((0, 0), (0, pad_t), (0, 0))), (0, 2, 1)
                       ).reshape(dec_batch, N_IDX_HEADS * SAMPLE_ROWS, 1)
    wx = jnp.broadcast_to(wx, (dec_batch, N_IDX_HEADS * SAMPLE_ROWS, LANES))

    def pad_new(a):
        return jnp.pad(a.reshape(dec_batch, dec_seq, a.shape[1]), ((0, 0), (0, LANES - dec_seq), (0, 0)))

    g_idx = 8 if n_pages % 8 == 0 else 1
    mask = _sample_index(page_table, cache_k_idx[layer], qx, wx, pad_new(ki16_s[:, :D_IDX]), dec_seq, g_idx)
    g_att = 4 if n_pages % 4 == 0 else 1
    o_s = _sample_attn(
        lam, lam_init, page_table,
        cache_kv_a[layer].reshape(n_phys, LANES, 2 * A_WIDTH),
        cache_k_b[layer].reshape(n_phys, LANES, B_WIDTH),
        cache_v_b[layer].reshape(n_phys, LANES, B_WIDTH),
        _block_diag_rows(qa_s.reshape(dec_batch, dec_seq, N_HEADS_A, HEAD_DIM)),
        _block_diag_rows(qb_s.reshape(dec_batch, dec_seq, 2 * N_HEADS_B, HEAD_DIM)),
        mask, _sample_bias(table_a, past, dec_seq, False), _sample_bias(table_b, past, dec_seq, True),
        pad_new(kva16_s), pad_new(kb16_s), pad_new(vb16_s), g_sub, g_att)
    o_s = o_s[:, :dec_seq].reshape(n_s, A_WIDTH + B_WIDTH)
    x1_s, qm_s = _tail1(x_sample.reshape(n_s, d), [(o_s, w_out16)], g_mem_l, wq16)
    x2_s, h_s, route_s = _tail2(qm_s.reshape(dec_batch, dec_seq, mem_w),
                                cache_mem_kv[layer].reshape(dec_batch, n_mem, 2 * mem_w),
                                x1_s.reshape(dec_batch, dec_seq, d), wo16, g_ffn_l, wr, br)
    y_sample = _moe_and_final(x2_s.reshape(n_s, d), h_s.reshape(n_s, d), route_s.reshape(n_s, LANES),
                              wgu16, wdn16, g_fin).reshape(dec_batch, dec_seq, d)

    return (y_prompt, y_sample,
            kva.reshape(1, batch, seq, 2, N_HEADS_A, HEAD_DIM),
            ki[:, :D_IDX].reshape(1, batch, seq, D_IDX),
            kb.reshape(1, batch, seq, N_HEADS_B, 2, HEAD_DIM),
            vb.reshape(1, batch, seq, N_HEADS_B, 2 * HEAD_DIM),
            mkv_p.reshape(1, batch, n_mem, 2, N_HEADS_MEM, HEAD_DIM),
            kva_s.reshape(1, dec_batch, dec_seq, 2, N_HEADS_A, HEAD_DIM),
            ki_s[:, :D_IDX].reshape(1, dec_batch, dec_seq, D_IDX),
            kb_s.reshape(1, dec_batch, dec_seq, N_HEADS_B, 2, HEAD_DIM),
            vb_s.reshape(1, dec_batch, dec_seq, N_HEADS_B, 2 * HEAD_DIM))
```

```python
import functools
import math

import jax
import jax.numpy as jnp
from jax import lax
from jax.experimental import pallas as pl
from jax.experimental.pallas import tpu as pltpu

F32 = jnp.float32
BF16 = jnp.bfloat16
I32 = jnp.int32

HEAD_DIM = 64
N_HEADS_A = 8
N_HEADS_B = 4
A_WIDTH = 512
B_WIDTH = 512
N_IDX_HEADS = 8
D_IDX = 64
TOPK_MAX = 256
N_HEADS_MEM = 4
N_BUCKETS = 32
MAX_EXACT = 16
MAX_DISTANCE = 128
N_GROUPS = 4
EXPERTS_PER_GROUP = 8
N_EXPERTS = 32
D_EXPERT = 512
NEG = -1e30
EPS = 1e-6
LANES = 128
INT_MIN = -(2 ** 31)

SEG_QA, SEG_KVA, SEG_QI, SEG_KI, SEG_WI, SEG_QB, SEG_KB, SEG_VB = (
    (0, 512), (512, 1536), (1536, 2048), (2048, 2176), (2176, 2304), (2304, 2816), (2816, 3328), (3328, 3840))
PACKED_COLS = 3840

VMEM_LIMIT = 56 * 1024 * 1024
ATTN_TILE = 256
EXPERT_ROWS = 256
MOE_CHUNK = 1024


def _cparams(sem):
    return pltpu.CompilerParams(dimension_semantics=sem, vmem_limit_bytes=VMEM_LIMIT)


def _dot(a, b):
    return jnp.dot(a, b, preferred_element_type=F32)


def _dot_nt(a, b):
    return lax.dot_general(a, b, (((1,), (1,)), ((), ())), preferred_element_type=F32)


def _rms(x, g):
    return x * lax.rsqrt(jnp.mean(x * x, axis=-1, keepdims=True) + EPS) * g


def _fold_lanes(x):
    acc = x[:, :LANES]
    for c in range(1, x.shape[1] // LANES):
        acc = acc + x[:, c * LANES:(c + 1) * LANES]
    return acc


def _sortable_key(score):
    bits = lax.bitcast_convert_type(score, I32)
    key = jnp.where(bits < 0, bits ^ jnp.int32(0x7FFFFFFF), bits)
    return jnp.where(bits == jnp.int32(INT_MIN), 0, key)


def _proj_kernel(x_ref, g_ref, w_ref, gk_ref, qa_ref, kva_ref, kva16_ref, qi_ref, ki_ref, ki16_ref, wi_ref,
                 qb_ref, kb_ref, vb_ref, kb16_ref, vb16_ref):
    h = _rms(x_ref[...], g_ref[...]).astype(BF16)

    def seg(s):
        return _dot(h, w_ref[:, s[0]:s[1]])

    qa_ref[...] = (seg(SEG_QA) * HEAD_DIM ** -0.5).astype(BF16)
    kva = seg(SEG_KVA)
    kva_ref[...] = kva
    kva16_ref[...] = kva.astype(BF16)
    qi_ref[...] = (seg(SEG_QI) * D_IDX ** -0.5).astype(BF16)
    ki = _rms(seg(SEG_KI), gk_ref[...])
    ki_ref[...] = ki
    ki16_ref[...] = ki.astype(BF16)
    wi_ref[...] = seg(SEG_WI) * N_IDX_HEADS ** -0.5
    qb_ref[...] = (seg(SEG_QB) * HEAD_DIM ** -0.5).astype(BF16)
    kb = seg(SEG_KB)
    kb_ref[...] = kb
    kb16_ref[...] = kb.astype(BF16)
    vb = seg(SEG_VB)
    vb_ref[...] = vb
    vb16_ref[...] = vb.astype(BF16)


def _proj(x, g, w_packed, gk2):
    n, d = x.shape
    tm = min(n, 512)
    widths = [(512, BF16), (1024, F32), (1024, BF16), (512, BF16), (128, F32), (128, BF16), (128, F32),
              (512, BF16), (512, F32), (512, F32), (512, BF16), (512, BF16)]
    return pl.pallas_call(
        _proj_kernel,
        grid=(n // tm,),
        in_specs=[pl.BlockSpec((tm, d), lambda i: (i, 0)),
                  pl.BlockSpec((1, d), lambda i: (0, 0)),
                  pl.BlockSpec((d, PACKED_COLS), lambda i: (0, 0)),
                  pl.BlockSpec((1, LANES), lambda i: (0, 0))],
        out_specs=[pl.BlockSpec((tm, w), lambda i: (i, 0)) for w, _ in widths],
        out_shape=[jax.ShapeDtypeStruct((n, w), dt) for w, dt in widths],
        compiler_params=_cparams(("arbitrary",)),
        name="proj",
    )(x, g, w_packed, gk2)


def _norm_matmul_kernel(x_ref, g_ref, w_ref, o_ref):
    o_ref[...] = _dot(_rms(x_ref[...], g_ref[...]).astype(BF16), w_ref[...])


def _norm_matmul(x, g, w16):
    n, d = x.shape
    tm = min(n, 512)
    return pl.pallas_call(
        _norm_matmul_kernel,
        grid=(n // tm,),
        in_specs=[pl.BlockSpec((tm, d), lambda i: (i, 0)),
                  pl.BlockSpec((1, d), lambda i: (0, 0)),
                  pl.BlockSpec(w16.shape, lambda i: (0, 0))],
        out_specs=pl.BlockSpec((tm, w16.shape[1]), lambda i: (i, 0)),
        out_shape=jax.ShapeDtypeStruct((n, w16.shape[1]), F32),
        compiler_params=_cparams(("arbitrary",)),
        name="norm_matmul",
    )(x, g, w16)


def _split_halves(q_ref, dst_ref, n_pairs):
    t = q_ref.shape[0]
    lo = lax.broadcasted_iota(I32, (t, LANES), 1) < HEAD_DIM
    for p in range(n_pairs):
        qp = q_ref[:, p * LANES:(p + 1) * LANES]
        dst_ref[2 * p] = jnp.where(lo, qp, jnp.zeros_like(qp))
        dst_ref[2 * p + 1] = jnp.where(lo, jnp.zeros_like(qp), qp)


def _online_softmax_step(s, h, m_ref, l_ref):
    m_prev = m_ref[h]
    m_new = jnp.maximum(m_prev, jnp.max(s, axis=1, keepdims=True))
    alpha = jnp.exp(m_prev - m_new)
    p = jnp.exp(s - jnp.concatenate([m_new] * (s.shape[1] // LANES), axis=1))
    l_ref[h] = alpha * l_ref[h] + jnp.sum(p, axis=1, keepdims=True)
    m_ref[h] = m_new
    return p, alpha


def _t5_bucket(dist):
    n = jnp.maximum(dist, 0)
    nf = jnp.maximum(n, 1).astype(F32)
    large = MAX_EXACT + (jnp.log(nf / MAX_EXACT) / math.log(MAX_DISTANCE / MAX_EXACT)
                         * (N_BUCKETS - MAX_EXACT)).astype(I32)
    large = jnp.minimum(large, N_BUCKETS - 1)
    return jnp.where(n < MAX_EXACT, n, large)


def _bias_lookup(table, dist):
    onehot = jax.nn.one_hot(_t5_bucket(dist), N_BUCKETS, dtype=F32)
    return jnp.dot(onehot, table.astype(F32), precision=lax.Precision.HIGHEST)


def _prompt_bias_tiles(table, t):
    assert t >= MAX_DISTANCE
    r = jnp.arange(t, dtype=I32)[:, None]
    c = jnp.arange(t, dtype=I32)[None, :]
    tiles = []
    for delta in range(3):
        dist = delta * t + r - c
        b = jnp.transpose(_bias_lookup(table, dist), (2, 0, 1))
        tiles.append(jnp.where(dist[None] >= 0, b, NEG))
    return jnp.stack(tiles)


def _dsa_prompt_kernel(k_sel, n_bits, qa_ref, qi_ref, wi_ref, k2_ref, kv_ref, bt_ref, o_ref,
                       qah_ref, qih_ref, key_ref, m_ref, l_ref, acc_ref):
    t = qa_ref.shape[0]
    i = pl.program_id(1)
    nj = i + 1
    row = lax.broadcasted_iota(I32, (t, t), 0)
    col = lax.broadcasted_iota(I32, (t, t), 1)
    lo = lax.broadcasted_iota(I32, (t, LANES), 1) < HEAD_DIM

    _split_halves(qa_ref, qah_ref, N_HEADS_A // 2)
    _split_halves(qi_ref, qih_ref, N_IDX_HEADS // 2)

    def score_chunk(j, carry):
        k0 = pl.multiple_of(j * t, t)
        kc = k2_ref[pl.ds(k0, t), :]
        wi = wi_ref[...]
        acc = jnp.zeros((t, t), F32)
        for h in range(N_IDX_HEADS):
            acc = acc + jnp.maximum(_dot_nt(qih_ref[h], kc), 0.0) * wi[:, h:h + 1]
        key = _sortable_key(acc)
        key_ref[j] = jnp.where(k0 + col <= i * t + row, key, jnp.int32(INT_MIN))
        return carry

    lax.fori_loop(0, nj, score_chunk, 0)

    def count_where(pred_fn):
        def body(j, c):
            return c + _fold_lanes(jnp.where(pred_fn(key_ref[j], j), 1.0, 0.0))
        c = lax.fori_loop(0, nj, body, jnp.zeros((t, LANES), F32))
        return jnp.sum(c, axis=1, keepdims=True)

    def bit_step(b, tu):
        cand_u = tu | lax.shift_left(jnp.int32(1), 31 - b)
        cand = cand_u ^ jnp.int32(INT_MIN)
        cnt = count_where(lambda kc, j: kc >= cand)
        return jnp.where(cnt >= k_sel, cand_u, tu)

    tu = lax.fori_loop(0, 32, bit_step, jnp.zeros((t, 1), I32))
    thr = tu ^ jnp.int32(INT_MIN)

    n_gt = count_where(lambda kc, j: kc > thr)
    n_eq = count_where(lambda kc, j: kc == thr)
    need = k_sel - n_gt
    tied = jnp.where((n_eq > need) & (thr != jnp.int32(INT_MIN)), 1.0, 0.0)

    def tie_search(_):
        def cut_step(b, c0):
            cand = c0 | lax.shift_left(jnp.int32(1), n_bits - 1 - b)
            cnt = count_where(lambda kc, j: (kc == thr) & (j * t + col < cand))
            return jnp.where(cnt < need, cand, c0)
        return lax.fori_loop(0, n_bits, cut_step, jnp.zeros((t, 1), I32))

    cut = lax.cond(jnp.max(tied) > 0.0, tie_search,
                   lambda _: jnp.full((t, 1), 2 ** 30, I32), 0)

    m_ref[...] = jnp.full(m_ref.shape, NEG, F32)
    l_ref[...] = jnp.zeros(l_ref.shape, F32)
    acc_ref[...] = jnp.zeros(acc_ref.shape, F32)

    def attn_chunk(j, carry):
        k0 = pl.multiple_of(j * t, t)
        kc = key_ref[j]
        sel = (kc > thr) | ((kc == thr) & (k0 + col <= cut))
        dsel = jnp.minimum(i - j, 2)
        for p in range(N_HEADS_A // 2):
            kp = kv_ref[pl.ds(k0, t), p * LANES:(p + 1) * LANES]
            vp = kv_ref[pl.ds(k0, t), A_WIDTH + p * LANES:A_WIDTH + (p + 1) * LANES]
            pv, al = [], []
            for half in range(2):
                h = 2 * p + half
                s = jnp.where(sel, _dot_nt(qah_ref[h], kp) + bt_ref[dsel, h], NEG)
                pr, alpha = _online_softmax_step(s, h, m_ref, l_ref)
                pv.append(_dot(pr.astype(BF16), vp))
                al.append(alpha)
            acc_ref[p] = acc_ref[p] * jnp.where(lo, al[0], al[1]) + jnp.where(lo, pv[0], pv[1])
        return carry

    lax.fori_loop(0, nj, attn_chunk, 0)

    for p in range(N_HEADS_A // 2):
        linv = jnp.where(lo, 1.0 / l_ref[2 * p], 1.0 / l_ref[2 * p + 1])
        o_ref[:, p * LANES:(p + 1) * LANES] = (acc_ref[p] * linv).astype(BF16)


def _dsa_prompt(qa, qi, wi, ki16, kva16, bias_tiles, batch, seq):
    t = min(ATTN_TILE, seq)
    nq = seq // t
    k_sel = min(TOPK_MAX, seq // 4)
    n_bits = max(1, (seq - 1).bit_length())
    kern = functools.partial(_dsa_prompt_kernel, k_sel, n_bits)
    return pl.pallas_call(
        kern,
        grid=(batch, nq),
        in_specs=[pl.BlockSpec((t, A_WIDTH), lambda b, i: (b * nq + i, 0)),
                  pl.BlockSpec((t, A_WIDTH), lambda b, i: (b * nq + i, 0)),
                  pl.BlockSpec((t, LANES), lambda b, i: (b * nq + i, 0)),
                  pl.BlockSpec((seq, LANES), lambda b, i: (b, 0)),
                  pl.BlockSpec((seq, 2 * A_WIDTH), lambda b, i: (b, 0)),
                  pl.BlockSpec(bias_tiles.shape, lambda b, i: (0, 0, 0, 0))],
        out_specs=pl.BlockSpec((t, A_WIDTH), lambda b, i: (b * nq + i, 0)),
        out_shape=jax.ShapeDtypeStruct((batch * seq, A_WIDTH), BF16),
        scratch_shapes=[pltpu.VMEM((N_HEADS_A, t, LANES), BF16),
                        pltpu.VMEM((N_IDX_HEADS, t, LANES), BF16),
                        pltpu.VMEM((nq, t, t), I32),
                        pltpu.VMEM((N_HEADS_A, t, LANES), F32),
                        pltpu.VMEM((N_HEADS_A, t, LANES), F32),
                        pltpu.VMEM((N_HEADS_A // 2, t, LANES), F32)],
        compiler_params=_cparams(("arbitrary", "arbitrary")),
        name="dsa_prompt",
    )(qa, qi, wi, ki16, kva16, bias_tiles)


def _subln(o0, o1, lam, lam_init, g):
    d = o0 - lam * o1
    return _rms(d, g) * (1.0 - lam_init)


def _diff_prompt_kernel(lam_init, lam_ref, qb_ref, kb_ref, vb_ref, bt_ref, g_ref, o_ref,
                        qh_ref, m_ref, l_ref, acc_ref):
    t = qb_ref.shape[0]
    i = pl.program_id(1)
    n_maps = 2 * N_HEADS_B
    _split_halves(qb_ref, qh_ref, N_HEADS_B)
    m_ref[...] = jnp.full(m_ref.shape, NEG, F32)
    l_ref[...] = jnp.zeros(l_ref.shape, F32)
    acc_ref[...] = jnp.zeros(acc_ref.shape, F32)

    def chunk(j, carry):
        k0 = pl.multiple_of(j * t, t)
        dsel = jnp.minimum(i - j, 2)
        for h in range(N_HEADS_B):
            kp = kb_ref[pl.ds(k0, t), h * LANES:(h + 1) * LANES]
            vp = vb_ref[pl.ds(k0, t), h * LANES:(h + 1) * LANES]
            for mp in range(2):
                c = 2 * h + mp
                s = _dot_nt(qh_ref[c], kp) + bt_ref[dsel, c]
                pr, alpha = _online_softmax_step(s, c, m_ref, l_ref)
                acc_ref[c] = acc_ref[c] * alpha + _dot(pr.astype(BF16), vp)
        return carry

    lax.fori_loop(0, i + 1, chunk, 0)

    lam = lam_ref[0]
    for h in range(N_HEADS_B):
        o0 = acc_ref[2 * h] / l_ref[2 * h]
        o1 = acc_ref[2 * h + 1] / l_ref[2 * h + 1]
        o_ref[:, h * LANES:(h + 1) * LANES] = _subln(o0, o1, lam, lam_init, g_ref[...]).astype(BF16)
    del n_maps


def _diff_prompt(lam, lam_init, qb, kb16, vb16, bias_tiles, g_sub, batch, seq):
    t = min(ATTN_TILE, seq)
    nq = seq // t
    n_maps = 2 * N_HEADS_B
    kern = functools.partial(_diff_prompt_kernel, lam_init)
    return pl.pallas_call(
        kern,
        grid=(batch, nq),
        in_specs=[pl.BlockSpec(memory_space=pltpu.SMEM),
                  pl.BlockSpec((t, B_WIDTH), lambda b, i: (b * nq + i, 0)),
                  pl.BlockSpec((seq, B_WIDTH), lambda b, i: (b, 0)),
                  pl.BlockSpec((seq, B_WIDTH), lambda b, i: (b, 0)),
                  pl.BlockSpec(bias_tiles.shape, lambda b, i: (0, 0, 0, 0)),
                  pl.BlockSpec((1, LANES), lambda b, i: (0, 0))],
        out_specs=pl.BlockSpec((t, B_WIDTH), lambda b, i: (b * nq + i, 0)),
        out_shape=jax.ShapeDtypeStruct((batch * seq, B_WIDTH), BF16),
        scratch_shapes=[pltpu.VMEM((n_maps, t, LANES), BF16),
                        pltpu.VMEM((n_maps, t, LANES), F32),
                        pltpu.VMEM((n_maps, t, LANES), F32),
                        pltpu.VMEM((n_maps, t, LANES), F32)],
        compiler_params=_cparams(("arbitrary", "arbitrary")),
        name="diff_prompt",
    )(lam, qb, kb16, vb16, bias_tiles, g_sub)


SAMPLE_ROWS = 8


def _sample_index_kernel(k_sel, n_pages, g_pages, dec_seq, n_bits, pt_ref, *refs):
    page_refs = refs[:g_pages]
    qx_ref, wx_ref, knew_ref, mask_ref, key_ref = refs[g_pages:]
    del pt_ref
    step = pl.program_id(1)
    n_steps = n_pages // g_pages
    r8 = SAMPLE_ROWS

    def chunk_scores(kt16):
        d = jnp.maximum(_dot(qx_ref[0], kt16), 0.0) * wx_ref[0]
        acc = d[0:r8]
        for h in range(1, N_IDX_HEADS):
            acc = acc + d[h * r8:(h + 1) * r8]
        return _sortable_key(acc)

    for g in range(g_pages):
        key_ref[step * g_pages + g] = chunk_scores(page_refs[g][0].astype(BF16))

    @pl.when(step == n_steps - 1)
    def _():
        row = lax.broadcasted_iota(I32, (r8, LANES), 0)
        lane = lax.broadcasted_iota(I32, (r8, LANES), 1)
        key_new = chunk_scores(knew_ref[0])
        key_ref[n_pages] = jnp.where((lane <= row) & (lane < dec_seq), key_new, jnp.int32(INT_MIN))

        n_chunks = n_pages + 1
        shape3 = (n_chunks, r8, LANES)
        colg = lax.broadcasted_iota(I32, shape3, 0) * LANES + lax.broadcasted_iota(I32, shape3, 2)

        def count(pred):
            c = jnp.sum(jnp.where(pred, 1.0, 0.0), axis=0)
            return jnp.sum(c, axis=1, keepdims=True)[None]

        def bit_step(b, tu):
            cand_u = tu | lax.shift_left(jnp.int32(1), 31 - b)
            cand = cand_u ^ jnp.int32(INT_MIN)
            return jnp.where(count(key_ref[...] >= cand) >= k_sel, cand_u, tu)

        tu = lax.fori_loop(0, 32, bit_step, jnp.zeros((1, r8, 1), I32))
        thr = tu ^ jnp.int32(INT_MIN)
        keys = key_ref[...]
        need = k_sel - count(keys > thr)

        def cut_step(b, c0):
            cand = c0 | lax.shift_left(jnp.int32(1), n_bits - 1 - b)
            return jnp.where(count((key_ref[...] == thr) & (colg < cand)) < need, cand, c0)

        cut = lax.fori_loop(0, n_bits, cut_step, jnp.zeros((1, r8, 1), I32))
        sel = (keys > thr) | ((keys == thr) & (colg <= cut))
        sel = sel & (keys != jnp.int32(INT_MIN))
        mask_ref[0] = jnp.where(sel, 0.0, NEG)


def _sample_index(page_table, cache_kidx_t, qx, wx, knew_t, dec_seq, g_pages):
    batch, n_pages = page_table.shape
    past = n_pages * LANES
    k_sel = min(TOPK_MAX, (past + dec_seq) // 4)
    n_bits = max(1, (past + LANES - 1).bit_length())
    n_steps = n_pages // g_pages
    rows = N_IDX_HEADS * SAMPLE_ROWS
    kern = functools.partial(_sample_index_kernel, k_sel, n_pages, g_pages, dec_seq, n_bits)
    page_specs = [pl.BlockSpec((1, D_IDX, LANES), functools.partial(
        lambda b, s, pt, g: (pt[b, s * g_pages + g], 0, 0), g=g)) for g in range(g_pages)]
    grid_spec = pltpu.PrefetchScalarGridSpec(
        num_scalar_prefetch=1,
        grid=(batch, n_steps),
        in_specs=page_specs + [pl.BlockSpec((1, rows, D_IDX), lambda b, s, pt: (b, 0, 0)),
                               pl.BlockSpec((1, rows, LANES), lambda b, s, pt: (b, 0, 0)),
                               pl.BlockSpec((1, D_IDX, LANES), lambda b, s, pt: (b, 0, 0))],
        out_specs=pl.BlockSpec((1, n_pages + 1, SAMPLE_ROWS, LANES), lambda b, s, pt: (b, 0, 0, 0)),
        scratch_shapes=[pltpu.VMEM((n_pages + 1, SAMPLE_ROWS, LANES), I32)])
    return pl.pallas_call(
        kern,
        grid_spec=grid_spec,
        out_shape=jax.ShapeDtypeStruct((batch, n_pages + 1, SAMPLE_ROWS, LANES), F32),
        compiler_params=_cparams(("arbitrary", "arbitrary")),
        name="sample_index",
    )(page_table, *([cache_kidx_t] * g_pages), qx, wx, knew_t)


def _sample_attn_kernel(lam_init, n_pages, g_pages, pt_ref, lam_ref, *refs):
    kva_refs = refs[:g_pages]
    kb_refs = refs[g_pages:2 * g_pages]
    vb_refs = refs[2 * g_pages:3 * g_pages]
    (qa_ref, qb_ref, mask_ref, maskn_ref, ba_ref, bb_ref, kvan_ref, kbn_ref, vbn_ref, g_ref,
     o_ref, m_ref, l_ref, acca_ref, accb_ref) = refs[3 * g_pages:]
    del pt_ref
    step = pl.program_id(1)
    n_steps = n_pages // g_pages
    r8 = SAMPLE_ROWS

    @pl.when(step == 0)
    def _():
        m_ref[...] = jnp.full(m_ref.shape, NEG, F32)
        l_ref[...] = jnp.zeros(l_ref.shape, F32)
        acca_ref[...] = jnp.zeros(acca_ref.shape, F32)
        accb_ref[...] = jnp.zeros(accb_ref.shape, F32)

    def both(kva_ref, kb_ref, vb_ref, mask8, bsel):
        mask = jnp.concatenate([mask8] * N_HEADS_A, axis=0)
        s = _dot(qa_ref[0], kva_ref[0, :A_WIDTH, :].astype(BF16)) + (ba_ref[bsel] + mask)
        pr, alpha = _online_softmax_step(s, 0, m_ref, l_ref)
        acca_ref[...] = acca_ref[...] * jnp.concatenate([alpha] * (A_WIDTH // LANES), axis=1) \
            + _dot_nt(pr.astype(BF16), kva_ref[0, A_WIDTH:, :].astype(BF16))
        s = _dot(qb_ref[0], kb_ref[0].astype(BF16)) + bb_ref[bsel]
        pr, alpha = _online_softmax_step(s, 1, m_ref, l_ref)
        pr = pr.astype(BF16)
        pv = [_dot(pr[2 * r8 * h:2 * r8 * (h + 1)],
                   vb_ref[0, pl.ds(h, LANES, stride=N_HEADS_B), :].astype(BF16))
              for h in range(N_HEADS_B)]
        accb_ref[...] = accb_ref[...] * alpha + jnp.concatenate(pv, axis=0)

    for g in range(g_pages):
        page = step * g_pages + g
        bsel = jnp.where(page == n_pages - 1, 1, 0)
        both(kva_refs[g], kb_refs[g], vb_refs[g], mask_ref[0, g], bsel)

    @pl.when(step == n_steps - 1)
    def _():
        both(kvan_ref, kbn_ref, vbn_ref, maskn_ref[0, 0], 2)
        lane = lax.broadcasted_iota(I32, (r8, A_WIDTH), 1)
        oa = jnp.zeros((r8, A_WIDTH), F32)
        la = jnp.concatenate([l_ref[0]] * (A_WIDTH // LANES), axis=1)
        acca = acca_ref[...] / la
        for h in range(N_HEADS_A):
            blk = acca[h * r8:(h + 1) * r8]
            oa = jnp.where((lane >= h * HEAD_DIM) & (lane < (h + 1) * HEAD_DIM), blk, oa)
        o_ref[0, :, :A_WIDTH] = oa.astype(BF16)
        lam = lam_ref[0]
        ob = accb_ref[...] / l_ref[1]
        for h in range(N_HEADS_B):
            r0 = 2 * h * r8
            o_ref[0, :, A_WIDTH + h * LANES:A_WIDTH + (h + 1) * LANES] = _subln(
                ob[r0:r0 + r8], ob[r0 + r8:r0 + 2 * r8], lam, lam_init, g_ref[...]).astype(BF16)


def _sample_attn(lam, lam_init, page_table, cache_kva_t, cache_kb_t, cache_vb, qa_x, qb_x, mask, bias_a, bias_b,
                 kva_new_t, kb_new_t, vb_new, g_sub, g_pages):
    batch, n_pages = page_table.shape
    n_steps = n_pages // g_pages
    rows = N_HEADS_A * SAMPLE_ROWS
    kern = functools.partial(_sample_attn_kernel, lam_init, n_pages, g_pages)

    def page_spec(width, g):
        return pl.BlockSpec((1, width, LANES), lambda b, s, pt: (pt[b, s * g_pages + g], 0, 0))

    in_specs = ([pl.BlockSpec(memory_space=pltpu.SMEM)]
                + [page_spec(2 * A_WIDTH, g) for g in range(g_pages)]
                + [page_spec(B_WIDTH, g) for g in range(g_pages)]
                + [page_spec(B_WIDTH, g) for g in range(g_pages)]
                + [pl.BlockSpec((1, rows, A_WIDTH), lambda b, s, pt: (b, 0, 0)),
                   pl.BlockSpec((1, rows, B_WIDTH), lambda b, s, pt: (b, 0, 0)),
                   pl.BlockSpec((1, g_pages, SAMPLE_ROWS, LANES), lambda b, s, pt: (b, s, 0, 0)),
                   pl.BlockSpec((1, 1, SAMPLE_ROWS, LANES), lambda b, s, pt: (b, n_pages, 0, 0)),
                   pl.BlockSpec(bias_a.shape, lambda b, s, pt: (0, 0, 0)),
                   pl.BlockSpec(bias_b.shape, lambda b, s, pt: (0, 0, 0)),
                   pl.BlockSpec((1, 2 * A_WIDTH, LANES), lambda b, s, pt: (b, 0, 0)),
                   pl.BlockSpec((1, B_WIDTH, LANES), lambda b, s, pt: (b, 0, 0)),
                   pl.BlockSpec((1, B_WIDTH, LANES), lambda b, s, pt: (b, 0, 0)),
                   pl.BlockSpec((1, LANES), lambda b, s, pt: (0, 0))])
    grid_spec = pltpu.PrefetchScalarGridSpec(
        num_scalar_prefetch=1,
        grid=(batch, n_steps),
        in_specs=in_specs,
        out_specs=pl.BlockSpec((1, SAMPLE_ROWS, A_WIDTH + B_WIDTH), lambda b, s, pt: (b, 0, 0)),
        scratch_shapes=[pltpu.VMEM((2, rows, LANES), F32),
                        pltpu.VMEM((2, rows, LANES), F32),
                        pltpu.VMEM((rows, A_WIDTH), F32),
                        pltpu.VMEM((rows, LANES), F32)])
    return pl.pallas_call(
        kern,
        grid_spec=grid_spec,
        out_shape=jax.ShapeDtypeStruct((batch, SAMPLE_ROWS, A_WIDTH + B_WIDTH), BF16),
        compiler_params=_cparams(("arbitrary", "arbitrary")),
        name="sample_attn",
    )(page_table, lam, *([cache_kva_t] * g_pages), *([cache_kb_t] * g_pages), *([cache_vb] * g_pages),
      qa_x, qb_x, mask, mask, bias_a, bias_b, kva_new_t, kb_new_t, vb_new, g_sub)


def _sample_bias(table, past, dec_seq, causal_new):
    t = jnp.arange(SAMPLE_ROWS, dtype=I32)[:, None]
    c = jnp.arange(LANES, dtype=I32)[None, :]
    far = jnp.full((SAMPLE_ROWS, LANES), MAX_DISTANCE, I32)
    last = t + LANES - c
    new = t - c
    out = []
    for dist in (far, last, new):
        b = jnp.transpose(_bias_lookup(table, dist), (2, 0, 1))
        out.append(b.reshape(table.shape[1] * SAMPLE_ROWS, LANES))
    if causal_new:
        ok = jnp.tile((new >= 0) & (c < dec_seq), (table.shape[1], 1))
        out[2] = jnp.where(ok, out[2], NEG)
    del past
    return jnp.stack(out)


def _block_diag_rows(q):
    b, t = q.shape[:2]
    q = jnp.pad(q, ((0, 0), (0, SAMPLE_ROWS - t), (0, 0), (0, 0)))
    q = jnp.transpose(q, (0, 2, 1, 3))
    eye = jnp.eye(8, dtype=q.dtype)
    return jnp.einsum('bctd,cg->bctgd', q, eye).reshape(b, 8 * SAMPLE_ROWS, 8 * HEAD_DIM)


def _tail1_kernel(n_mix, x_ref, *refs):
    o_refs = refs[:n_mix]
    w_refs = refs[n_mix:2 * n_mix]
    g_ref, wq_ref, x1_ref, qm_ref = refs[2 * n_mix:]
    x1 = x_ref[...]
    for o_ref, w_ref in zip(o_refs, w_refs):
        x1 = x1 + _dot(o_ref[...], w_ref[...])
    x1_ref[...] = x1
    hm = _rms(x1, g_ref[...]).astype(BF16)
    qm_ref[...] = (_dot(hm, wq_ref[...]) * HEAD_DIM ** -0.5).astype(BF16)


def _tail1(x, mixes, g_mem, wq16):
    n, d = x.shape
    tm = min(n, 512)
    n_mix = len(mixes)
    mw = wq16.shape[1]
    return pl.pallas_call(
        functools.partial(_tail1_kernel, n_mix),
        grid=(n // tm,),
        in_specs=([pl.BlockSpec((tm, d), lambda i: (i, 0))]
                  + [pl.BlockSpec((tm, o.shape[1]), lambda i: (i, 0)) for o, _ in mixes]
                  + [pl.BlockSpec(w.shape, lambda i: (0, 0)) for _, w in mixes]
                  + [pl.BlockSpec((1, d), lambda i: (0, 0)),
                     pl.BlockSpec(wq16.shape, lambda i: (0, 0))]),
        out_specs=[pl.BlockSpec((tm, d), lambda i: (i, 0)), pl.BlockSpec((tm, mw), lambda i: (i, 0))],
        out_shape=[jax.ShapeDtypeStruct((n, d), F32), jax.ShapeDtypeStruct((n, mw), BF16)],
        compiler_params=_cparams(("arbitrary",)),
        name="tail1",
    )(x, *[o for o, _ in mixes], *[w for _, w in mixes], g_mem, wq16)


def _tail2_kernel(qm_ref, mkv_ref, x1_ref, wo_ref, g_ref, wr_ref, br_ref, x2_ref, h_ref, route_ref):
    tq = qm_ref.shape[1]
    mem_w = N_HEADS_MEM * HEAD_DIM
    lo = lax.broadcasted_iota(I32, (tq, LANES), 1) < HEAD_DIM
    mkv = mkv_ref[0].astype(BF16)
    outs = []
    for p in range(N_HEADS_MEM // 2):
        qp = qm_ref[0, :, p * LANES:(p + 1) * LANES]
        kp = mkv[:, p * LANES:(p + 1) * LANES]
        vp = mkv[:, mem_w + p * LANES:mem_w + (p + 1) * LANES]
        pv = []
        for half in range(2):
            qh = jnp.where(lo == (half == 0), qp, jnp.zeros_like(qp))
            s = _dot_nt(qh, kp)
            e = jnp.exp(s - jnp.max(s, axis=1, keepdims=True))
            pv.append(_dot(e.astype(BF16), vp) / jnp.sum(e, axis=1, keepdims=True))
        outs.append(jnp.where(lo, pv[0], pv[1]))
    om = jnp.concatenate(outs, axis=1).astype(BF16)
    x2 = x1_ref[0] + _dot(om, wo_ref[...])
    x2_ref[0] = x2
    h = _rms(x2, g_ref[...])
    h_ref[0] = h.astype(BF16)

    lg = jnp.dot(h, wr_ref[...], preferred_element_type=F32, precision=lax.Precision.HIGHEST) + br_ref[...]
    lane = lax.broadcasted_iota(I32, (tq, LANES), 1)
    lane_f = lane.astype(F32)
    big = jnp.float32(1e9)
    is_g = lane < N_GROUPS
    m1 = jnp.max(jnp.where(is_g, lg, -jnp.inf), axis=1, keepdims=True)
    grp = jnp.min(jnp.where(is_g & (lg == m1), lane_f, big), axis=1, keepdims=True)
    p_grp = 1.0 / jnp.sum(jnp.where(is_g, jnp.exp(lg - m1), 0.0), axis=1, keepdims=True)
    e0 = N_GROUPS + grp * EXPERTS_PER_GROUP
    in_g = (lane_f >= e0) & (lane_f < e0 + EXPERTS_PER_GROUP)
    v1 = jnp.max(jnp.where(in_g, lg, -jnp.inf), axis=1, keepdims=True)
    i1 = jnp.min(jnp.where(in_g & (lg == v1), lane_f, big), axis=1, keepdims=True)
    rest = in_g & (lane_f != i1)
    v2 = jnp.max(jnp.where(rest, lg, -jnp.inf), axis=1, keepdims=True)
    i2 = jnp.min(jnp.where(rest & (lg == v2), lane_f, big), axis=1, keepdims=True)
    e2 = jnp.exp(v2 - v1)
    den = 1.0 + e2
    g1 = (1.0 / den) * p_grp
    g2 = (e2 / den) * p_grp
    route = jnp.where(lane == 0, i1 - N_GROUPS, 0.0)
    route = jnp.where(lane == 1, i2 - N_GROUPS, route)
    route = jnp.where(lane == 2, g1, route)
    route = jnp.where(lane == 3, g2, route)
    route_ref[0] = route


def _tail2(qm, mkv, x1, wo16, g_ffn, wr, br):
    b, t, d = x1.shape
    tq = min(t, 512)
    mem_w = qm.shape[2]
    n_mem = mkv.shape[1]
    return pl.pallas_call(
        _tail2_kernel,
        grid=(b, t // tq),
        in_specs=[pl.BlockSpec((1, tq, mem_w), lambda i, j: (i, j, 0)),
                  pl.BlockSpec((1, n_mem, 2 * mem_w), lambda i, j: (i, 0, 0)),
                  pl.BlockSpec((1, tq, d), lambda i, j: (i, j, 0)),
                  pl.BlockSpec(wo16.shape, lambda i, j: (0, 0)),
                  pl.BlockSpec((1, d), lambda i, j: (0, 0)),
                  pl.BlockSpec(wr.shape, lambda i, j: (0, 0)),
                  pl.BlockSpec((1, LANES), lambda i, j: (0, 0))],
        out_specs=[pl.BlockSpec((1, tq, d), lambda i, j: (i, j, 0)),
                   pl.BlockSpec((1, tq, d), lambda i, j: (i, j, 0)),
                   pl.BlockSpec((1, tq, LANES), lambda i, j: (i, j, 0))],
        out_shape=[jax.ShapeDtypeStruct((b, t, d), F32), jax.ShapeDtypeStruct((b, t, d), BF16),
                   jax.ShapeDtypeStruct((b, t, LANES), F32)],
        compiler_params=_cparams(("arbitrary", "arbitrary")),
        name="tail2",
    )(qm, mkv, x1, wo16, g_ffn, wr, br)


FLAG_FIRST, FLAG_LAST, FLAG_VALID = 1, 2, 4


def _moe_ffn_kernel(pb_ref, pc_ref, fl_ref, be_ref, h_ref, rt_ref, gate_ref, wgu_ref, wdn_ref, y_ref, acc_ref):
    del pb_ref, be_ref
    p = pl.program_id(0)
    fl = fl_ref[p]
    c = h_ref.shape[0]
    r = acc_ref.shape[0]

    @pl.when((fl & FLAG_FIRST) != 0)
    def _():
        acc_ref[...] = jnp.zeros(acc_ref.shape, F32)

    @pl.when((fl & FLAG_VALID) != 0)
    def _():
        rel = rt_ref[...] - pc_ref[p] * c
        onehot = jnp.where(rel == lax.broadcasted_iota(I32, (r, c), 1), 1.0, 0.0).astype(BF16)
        acc_ref[...] += _dot(onehot, h_ref[...])

    @pl.when((fl & FLAG_LAST) != 0)
    def _():
        gu = _dot(acc_ref[...].astype(BF16), wgu_ref[0])
        gate = gu[:, :D_EXPERT]
        up = gu[:, D_EXPERT:]
        act = gate * (1.0 / (1.0 + jnp.exp(-gate))) * up
        y_ref[...] = (_dot(act.astype(BF16), wdn_ref[0]) * gate_ref[...]).astype(BF16)


def _moe_ffn(pb, pc, flags, block_expert, h16, row_tok, row_gate, wgu16, wdn16, chunk):
    n_rows = row_tok.shape[0]
    d = h16.shape[1]
    r = EXPERT_ROWS
    grid_spec = pltpu.PrefetchScalarGridSpec(
        num_scalar_prefetch=4,
        grid=(pb.shape[0],),
        in_specs=[pl.BlockSpec((chunk, d), lambda p, pb, pc, fl, be: (pc[p], 0)),
                  pl.BlockSpec((r, 1), lambda p, pb, pc, fl, be: (pb[p], 0)),
                  pl.BlockSpec((r, 1), lambda p, pb, pc, fl, be: (pb[p], 0)),
                  pl.BlockSpec((1,) + wgu16.shape[1:], lambda p, pb, pc, fl, be: (be[pb[p]], 0, 0)),
                  pl.BlockSpec((1,) + wdn16.shape[1:], lambda p, pb, pc, fl, be: (be[pb[p]], 0, 0))],
        out_specs=pl.BlockSpec((r, d), lambda p, pb, pc, fl, be: (pb[p], 0)),
        scratch_shapes=[pltpu.VMEM((r, d), F32)])
    return pl.pallas_call(
        _moe_ffn_kernel,
        grid_spec=grid_spec,
        out_shape=jax.ShapeDtypeStruct((n_rows, d), BF16),
        compiler_params=_cparams(("arbitrary",)),
        name="moe_ffn",
    )(pb, pc, flags, block_expert, h16, row_tok.reshape(n_rows, 1), row_gate.reshape(n_rows, 1), wgu16, wdn16)


def _moe_combine_kernel(pb_ref, pc_ref, fl_ref, y_ref, rt_ref, x2_ref, g_ref, o_ref, acc_ref):
    del pb_ref
    p = pl.program_id(0)
    fl = fl_ref[p]
    c = acc_ref.shape[0]
    r = y_ref.shape[0]

    @pl.when((fl & FLAG_FIRST) != 0)
    def _():
        acc_ref[...] = jnp.zeros(acc_ref.shape, F32)

    @pl.when((fl & FLAG_VALID) != 0)
    def _():
        rel = rt_ref[0] - pc_ref[p] * c
        onehot_t = jnp.where(rel == lax.broadcasted_iota(I32, (c, r), 0), 1.0, 0.0).astype(BF16)
        acc_ref[...] += _dot(onehot_t, y_ref[...])

    @pl.when((fl & FLAG_LAST) != 0)
    def _():
        o_ref[...] = _rms(x2_ref[...] + acc_ref[...], g_ref[...])


def _moe_combine(pb, pc, flags, y_rows, row_tok, x2, g_final, chunk):
    n, d = x2.shape
    r = EXPERT_ROWS
    n_blocks = y_rows.shape[0] // r
    grid_spec = pltpu.PrefetchScalarGridSpec(
        num_scalar_prefetch=3,
        grid=(pb.shape[0],),
        in_specs=[pl.BlockSpec((r, d), lambda p, pb, pc, fl: (pb[p], 0)),
                  pl.BlockSpec((1, 1, r), lambda p, pb, pc, fl: (pb[p], 0, 0)),
                  pl.BlockSpec((chunk, d), lambda p, pb, pc, fl: (pc[p], 0)),
                  pl.BlockSpec((1, d), lambda p, pb, pc, fl: (0, 0))],
        out_specs=pl.BlockSpec((chunk, d), lambda p, pb, pc, fl: (pc[p], 0)),
        scratch_shapes=[pltpu.VMEM((chunk, d), F32)])
    return pl.pallas_call(
        _moe_combine_kernel,
        grid_spec=grid_spec,
        out_shape=jax.ShapeDtypeStruct((n, d), F32),
        compiler_params=_cparams(("arbitrary",)),
        name="moe_combine",
    )(pb, pc, flags, y_rows, row_tok.reshape(n_blocks, 1, r), x2, g_final)


def _moe_and_final(x2, h16, route, wgu16, wdn16, g_final):
    n, d = x2.shape
    n_assign = 2 * n
    experts = route[:, :2].astype(I32).reshape(n_assign)
    onehot = (experts[:, None] == jnp.arange(N_EXPERTS, dtype=I32)[None, :]).astype(I32)
    running = jnp.cumsum(onehot, axis=0)
    rank = jnp.sum((running - onehot) * onehot, axis=1)
    counts = running[-1]
    padded = (counts + EXPERT_ROWS - 1) // EXPERT_ROWS * EXPERT_ROWS
    pad_ends = jnp.cumsum(padded)
    pad_starts = pad_ends - padded
    pos = jnp.sum(onehot * pad_starts[None, :], axis=1) + rank
    n_blocks = -(-n_assign // EXPERT_ROWS) + N_EXPERTS
    n_rows = n_blocks * EXPERT_ROWS
    tok = jnp.arange(n_assign, dtype=I32) // 2
    row_tok = jnp.full((n_rows,), -1, I32).at[pos].set(tok)
    row_gate = jnp.zeros((n_rows,), F32).at[pos].set(route[:, 2:4].reshape(n_assign))
    block_expert = jnp.minimum(
        jnp.searchsorted(pad_ends, jnp.arange(n_blocks, dtype=I32) * EXPERT_ROWS, side='right'),
        N_EXPERTS - 1).astype(I32)

    chunk = min(MOE_CHUNK, n)
    assert n % chunk == 0
    n_chunks = n // chunk
    rt2 = row_tok.reshape(n_blocks, EXPERT_ROWS)
    hi = jnp.max(rt2, axis=1)
    lo = jnp.min(jnp.where(rt2 >= 0, rt2, n), axis=1)
    lo_c = jnp.where(hi < 0, 0, lo // chunk)
    hi_c = jnp.where(hi < 0, 0, hi // chunk)
    n_ch = hi_c - lo_c + 1
    ends = jnp.cumsum(n_ch)
    starts = ends - n_ch
    total = ends[-1]
    n_pairs = N_EXPERTS * n_chunks + n_blocks
    p = jnp.arange(n_pairs, dtype=I32)
    valid = p < total
    pb = jnp.minimum(jnp.searchsorted(ends, p, side='right'), n_blocks - 1).astype(I32)
    pc = jnp.where(valid, lo_c[pb] + p - starts[pb], hi_c[n_blocks - 1]).astype(I32)
    flags = (jnp.where(valid & (p == starts[pb]), FLAG_FIRST, 0) + jnp.where(valid & (p == ends[pb] - 1), FLAG_LAST, 0)
             + jnp.where(valid, FLAG_VALID, 0)).astype(I32)
    y_rows = _moe_ffn(pb, pc, flags, block_expert, h16, row_tok, row_gate, wgu16, wdn16, chunk)

    order = jnp.argsort(jnp.where(valid, pc * n_blocks + pb, jnp.int32(2 ** 30)))
    pb2, pc2 = pb[order], pc[order]
    last_valid = jnp.maximum(total - 1, 0)
    pb2 = jnp.where(valid, pb2, pb2[last_valid])
    pc2 = jnp.where(valid, pc2, pc2[last_valid])
    prev_c = jnp.concatenate([jnp.full((1,), -1, I32), pc2[:-1]])
    next_c = jnp.concatenate([pc2[1:], jnp.full((1,), -1, I32)])
    flags2 = (jnp.where(valid & (pc2 != prev_c), FLAG_FIRST, 0)
              + jnp.where(valid & ((pc2 != next_c) | (p == total - 1)), FLAG_LAST, 0)
              + jnp.where(valid, FLAG_VALID, 0)).astype(I32)
    return _moe_combine(pb2, pc2, flags2, y_rows, row_tok, x2, g_final, chunk)


def kernel(x_prompt, x_sample, cache_kv_a, cache_k_idx, cache_k_b, cache_v_b, cache_mem_kv, page_table, mem_prompt, rel_bias, g_mix, w_in, g_kidx, lambda_q1, lambda_k1, lambda_q2, lambda_k2, g_subln, w_out, g_mem, g_memin, w_mem_q, w_mem_kv, w_mem_o, g_ffn, w_router1, b_router1, w_router2, b_router2, w_gate_up, w_down, g_final):
    depth = g_mix.shape[0]
    assert depth == 1
    layer = 0
    batch, seq, d = x_prompt.shape
    dec_batch, dec_seq, _ = x_sample.shape
    n_pages = page_table.shape[1]
    n_phys = cache_kv_a.shape[1]
    past = n_pages * LANES
    n_mem = mem_prompt.shape[1]
    mem_w = N_HEADS_MEM * HEAD_DIM
    assert cache_kv_a.shape[2] == LANES and dec_seq <= SAMPLE_ROWS

    table_a = rel_bias[:, :N_HEADS_A]
    table_b = rel_bias[:, N_HEADS_A:]
    lam_init = 0.8 - 0.6 * math.exp(-0.3 * layer)
    lam = (jnp.exp(jnp.sum(lambda_q1[layer] * lambda_k1[layer]).astype(F32))
           - jnp.exp(jnp.sum(lambda_q2[layer] * lambda_k2[layer]).astype(F32)) + lam_init).reshape(1)

    w = w_in[layer]
    offs = [0, 512, 1024, 1536, 2048, 2112, 2120, 2632, 3144, 3656]
    q_a, k_a, v_a, q_i, k_i, w_i, q_b, k_b, v_b = [w[:, offs[s]:offs[s + 1]] for s in range(9)]
    w_packed = jnp.concatenate(
        [q_a, k_a, v_a, q_i, k_i, k_i, jnp.pad(w_i, ((0, 0), (0, LANES - N_IDX_HEADS))), q_b, k_b, v_b],
        axis=1).astype(BF16)
    gk2 = jnp.concatenate([g_kidx[layer], g_kidx[layer]]).reshape(1, LANES)
    g_mix_l = g_mix[layer].reshape(1, d)
    w_out16 = w_out[layer].astype(BF16)
    wq16 = w_mem_q[layer].astype(BF16)
    wo16 = w_mem_o[layer].astype(BF16)
    wr = jnp.concatenate([w_router1[layer],
                          jnp.transpose(w_router2[layer], (1, 0, 2)).reshape(d, N_EXPERTS),
                          jnp.zeros((d, LANES - N_GROUPS - N_EXPERTS), F32)], axis=1)
    br = jnp.concatenate([b_router1[layer], b_router2[layer].reshape(N_EXPERTS),
                          jnp.zeros((LANES - N_GROUPS - N_EXPERTS,), F32)]).reshape(1, LANES)
    wgu16 = w_gate_up[layer].astype(BF16)
    wdn16 = w_down[layer].astype(BF16)
    g_sub = g_subln[layer].reshape(1, LANES)
    g_mem_l = g_mem[layer].reshape(1, d)
    g_ffn_l = g_ffn[layer].reshape(1, d)
    g_fin = g_final.reshape(1, d)

    n_p = batch * seq
    (qa, kva, kva16, qi, ki, ki16, wi, qb, kb, vb, kb16, vb16) = _proj(
        x_prompt.reshape(n_p, d), g_mix_l, w_packed, gk2)
    t_attn = min(ATTN_TILE, seq)
    o_a = _dsa_prompt(qa, qi, wi, ki16, kva16, _prompt_bias_tiles(table_a, t_attn), batch, seq)
    o_b = _diff_prompt(lam, lam_init, qb, kb16, vb16, _prompt_bias_tiles(table_b, t_attn), g_sub, batch, seq)
    mkv_p = _norm_matmul(mem_prompt.reshape(batch * n_mem, d), g_memin[layer].reshape(1, d),
                         w_mem_kv[layer].astype(BF16))
    x1, qm = _tail1(x_prompt.reshape(n_p, d), [(o_a, w_out16[:A_WIDTH]), (o_b, w_out16[A_WIDTH:])],
                    g_mem_l, wq16)
    x2, h_ffn, route = _tail2(qm.reshape(batch, seq, mem_w), mkv_p.reshape(batch, n_mem, 2 * mem_w),
                              x1.reshape(batch, seq, d), wo16, g_ffn_l, wr, br)
    y_prompt = _moe_and_final(x2.reshape(n_p, d), h_ffn.reshape(n_p, d), route.reshape(n_p, LANES),
                              wgu16, wdn16, g_fin).reshape(batch, seq, d)

    n_s = dec_batch * dec_seq
    (qa_s, kva_s, kva16_s, qi_s, ki_s, ki16_s, wi_s, qb_s, kb_s, vb_s, kb16_s, vb16_s) = _proj(
        x_sample.reshape(n_s, d), g_mix_l, w_packed, gk2)
    pad_t = SAMPLE_ROWS - dec_seq
    qx = jnp.transpose(jnp.pad(qi_s.reshape(dec_batch, dec_seq, N_IDX_HEADS, D_IDX),
                               ((0, 0), (0, pad_t), (0, 0), (0, 0))), (0, 2, 1, 3)
                       ).reshape(dec_batch, N_IDX_HEADS * SAMPLE_ROWS, D_IDX)
    wx = jnp.transpose(jnp.pad(wi_s[:, :N_IDX_HEADS].reshape(dec_batch, dec_seq, N_IDX_HEADS),
                               ((0, 0), (0, pad_t), (0, 0))), (0, 2, 1)
                       ).reshape(dec_batch, N_IDX_HEADS * SAMPLE_ROWS, 1)
    wx = jnp.broadcast_to(wx, (dec_batch, N_IDX_HEADS * SAMPLE_ROWS, LANES))

    def new_chunk_t(a):
        a = jnp.pad(a.reshape(dec_batch, dec_seq, a.shape[1]), ((0, 0), (0, LANES - dec_seq), (0, 0)))
        return jnp.transpose(a, (0, 2, 1))

    kidx_t = jnp.transpose(cache_k_idx[layer], (0, 2, 1))
    kva_t = jnp.transpose(cache_kv_a[layer], (0, 2, 3, 4, 1)).reshape(n_phys, 2 * A_WIDTH, LANES)
    kb_t = jnp.transpose(cache_k_b[layer], (0, 2, 3, 4, 1)).reshape(n_phys, B_WIDTH, LANES)
    vb_r = cache_v_b[layer].reshape(n_phys, LANES * N_HEADS_B, 2 * HEAD_DIM)
    vb_new = jnp.pad(vb_s.reshape(dec_batch, dec_seq, N_HEADS_B, 2 * HEAD_DIM),
                     ((0, 0), (0, LANES - dec_seq), (0, 0), (0, 0))).reshape(dec_batch, LANES * N_HEADS_B, 2 * HEAD_DIM)

    g_idx = 8 if n_pages % 8 == 0 else 1
    mask = _sample_index(page_table, kidx_t, qx, wx, new_chunk_t(ki16_s[:, :D_IDX]), dec_seq, g_idx)
    g_att = 4 if n_pages % 4 == 0 else 1
    o_s = _sample_attn(
        lam, lam_init, page_table, kva_t, kb_t, vb_r,
        _block_diag_rows(qa_s.reshape(dec_batch, dec_seq, N_HEADS_A, HEAD_DIM)),
        _block_diag_rows(qb_s.reshape(dec_batch, dec_seq, 2 * N_HEADS_B, HEAD_DIM)),
        mask, _sample_bias(table_a, past, dec_seq, False), _sample_bias(table_b, past, dec_seq, True),
        new_chunk_t(kva16_s), new_chunk_t(kb16_s), vb_new, g_sub, g_att)
    o_s = o_s[:, :dec_seq].reshape(n_s, A_WIDTH + B_WIDTH)
    x1_s, qm_s = _tail1(x_sample.reshape(n_s, d), [(o_s, w_out16)], g_mem_l, wq16)
    x2_s, h_s, route_s = _tail2(qm_s.reshape(dec_batch, dec_seq, mem_w),
                                cache_mem_kv[layer].reshape(dec_batch, n_mem, 2 * mem_w),
                                x1_s.reshape(dec_batch, dec_seq, d), wo16, g_ffn_l, wr, br)
    y_sample = _moe_and_final(x2_s.reshape(n_s, d), h_s.reshape(n_s, d), route_s.reshape(n_s, LANES),
                              wgu16, wdn16, g_fin).reshape(dec_batch, dec_seq, d)

    return (y_prompt, y_sample,
            kva.reshape(1, batch, seq, 2, N_HEADS_A, HEAD_DIM),
            ki[:, :D_IDX].reshape(1, batch, seq, D_IDX),
            kb.reshape(1, batch, seq, N_HEADS_B, 2, HEAD_DIM),
            vb.reshape(1, batch, seq, N_HEADS_B, 2 * HEAD_DIM),
            mkv_p.reshape(1, batch, n_mem, 2, N_HEADS_MEM, HEAD_DIM),
            kva_s.reshape(1, dec_batch, dec_seq, 2, N_HEADS_A, HEAD_DIM),
            ki_s[:, :D_IDX].reshape(1, dec_batch, dec_seq, D_IDX),
            kb_s.reshape(1, dec_batch, dec_seq, N_HEADS_B, 2, HEAD_DIM),
            vb_s.reshape(1, dec_batch, dec_seq, N_HEADS_B, 2 * HEAD_DIM))
```

```python
import functools
import math

import jax
import jax.numpy as jnp
from jax import lax
from jax.experimental import pallas as pl
from jax.experimental.pallas import tpu as pltpu

F32 = jnp.float32
BF16 = jnp.bfloat16
I32 = jnp.int32

HEAD_DIM = 64
N_HEADS_A = 8
N_HEADS_B = 4
A_WIDTH = 512
B_WIDTH = 512
N_IDX_HEADS = 8
D_IDX = 64
TOPK_MAX = 256
N_HEADS_MEM = 4
N_BUCKETS = 32
MAX_EXACT = 16
MAX_DISTANCE = 128
N_GROUPS = 4
EXPERTS_PER_GROUP = 8
N_EXPERTS = 32
D_EXPERT = 512
NEG = -1e30
EPS = 1e-6
LANES = 128
INT_MIN = -(2 ** 31)

SEG_QA, SEG_KVA, SEG_QI, SEG_KI, SEG_WI, SEG_QB, SEG_KB, SEG_VB = (
    (0, 512), (512, 1536), (1536, 2048), (2048, 2176), (2176, 2304), (2304, 2816), (2816, 3328), (3328, 3840))
PACKED_COLS = 3840

VMEM_LIMIT = 56 * 1024 * 1024
ATTN_TILE = 256
SEARCH_ROWS = 64
EXPERT_ROWS = 256
MOE_CHUNK = 1024


def _cparams(sem):
    return pltpu.CompilerParams(dimension_semantics=sem, vmem_limit_bytes=VMEM_LIMIT)


def _dot(a, b):
    return jnp.dot(a, b, preferred_element_type=F32)


def _dot_nt(a, b):
    return lax.dot_general(a, b, (((1,), (1,)), ((), ())), preferred_element_type=F32)


def _rms(x, g):
    return x * lax.rsqrt(jnp.mean(x * x, axis=-1, keepdims=True) + EPS) * g


def _fold_lanes(x):
    acc = x[:, :LANES]
    for c in range(1, x.shape[1] // LANES):
        acc = acc + x[:, c * LANES:(c + 1) * LANES]
    return acc


def _sortable_key(score):
    bits = lax.bitcast_convert_type(score, I32)
    key = jnp.where(bits < 0, bits ^ jnp.int32(0x7FFFFFFF), bits)
    return jnp.where(bits == jnp.int32(INT_MIN), 0, key)


def _proj_kernel(x_ref, g_ref, w_ref, gk_ref, qa_ref, kva_ref, kva16_ref, qi_ref, ki_ref, ki16_ref, wi_ref,
                 qb_ref, kb_ref, vb_ref, kb16_ref, vb16_ref):
    h = _rms(x_ref[...], g_ref[...]).astype(BF16)

    def seg(s):
        return _dot(h, w_ref[:, s[0]:s[1]])

    qa_ref[...] = (seg(SEG_QA) * HEAD_DIM ** -0.5).astype(BF16)
    kva = seg(SEG_KVA)
    kva_ref[...] = kva
    kva16_ref[...] = kva.astype(BF16)
    qi_ref[...] = (seg(SEG_QI) * D_IDX ** -0.5).astype(BF16)
    ki = _rms(seg(SEG_KI), gk_ref[...])
    ki_ref[...] = ki
    ki16_ref[...] = ki.astype(BF16)
    wi_ref[...] = seg(SEG_WI) * N_IDX_HEADS ** -0.5
    qb_ref[...] = (seg(SEG_QB) * HEAD_DIM ** -0.5).astype(BF16)
    kb = seg(SEG_KB)
    kb_ref[...] = kb
    kb16_ref[...] = kb.astype(BF16)
    vb = seg(SEG_VB)
    vb_ref[...] = vb
    vb16_ref[...] = vb.astype(BF16)


def _proj(x, g, w_packed, gk2):
    n, d = x.shape
    tm = min(n, 512)
    widths = [(512, BF16), (1024, F32), (1024, BF16), (512, BF16), (128, F32), (128, BF16), (128, F32),
              (512, BF16), (512, F32), (512, F32), (512, BF16), (512, BF16)]
    return pl.pallas_call(
        _proj_kernel,
        grid=(n // tm,),
        in_specs=[pl.BlockSpec((tm, d), lambda i: (i, 0)),
                  pl.BlockSpec((1, d), lambda i: (0, 0)),
                  pl.BlockSpec((d, PACKED_COLS), lambda i: (0, 0)),
                  pl.BlockSpec((1, LANES), lambda i: (0, 0))],
        out_specs=[pl.BlockSpec((tm, w), lambda i: (i, 0)) for w, _ in widths],
        out_shape=[jax.ShapeDtypeStruct((n, w), dt) for w, dt in widths],
        compiler_params=_cparams(("arbitrary",)),
        name="proj",
    )(x, g, w_packed, gk2)


def _norm_matmul_kernel(x_ref, g_ref, w_ref, o_ref):
    o_ref[...] = _dot(_rms(x_ref[...], g_ref[...]).astype(BF16), w_ref[...])


def _norm_matmul(x, g, w16):
    n, d = x.shape
    tm = min(n, 512)
    return pl.pallas_call(
        _norm_matmul_kernel,
        grid=(n // tm,),
        in_specs=[pl.BlockSpec((tm, d), lambda i: (i, 0)),
                  pl.BlockSpec((1, d), lambda i: (0, 0)),
                  pl.BlockSpec(w16.shape, lambda i: (0, 0))],
        out_specs=pl.BlockSpec((tm, w16.shape[1]), lambda i: (i, 0)),
        out_shape=jax.ShapeDtypeStruct((n, w16.shape[1]), F32),
        compiler_params=_cparams(("arbitrary",)),
        name="norm_matmul",
    )(x, g, w16)


def _split_halves(q_ref, dst_ref, n_pairs):
    t = q_ref.shape[0]
    lo = lax.broadcasted_iota(I32, (t, LANES), 1) < HEAD_DIM
    for p in range(n_pairs):
        qp = q_ref[:, p * LANES:(p + 1) * LANES]
        dst_ref[2 * p] = jnp.where(lo, qp, jnp.zeros_like(qp))
        dst_ref[2 * p + 1] = jnp.where(lo, jnp.zeros_like(qp), qp)


def _online_softmax_step(s, h, m_ref, l_ref):
    m_prev = m_ref[h]
    m_new = jnp.maximum(m_prev, jnp.max(s, axis=1, keepdims=True))
    alpha = jnp.exp(m_prev - m_new)
    p = jnp.exp(s - jnp.concatenate([m_new] * (s.shape[1] // LANES), axis=1))
    l_ref[h] = alpha * l_ref[h] + jnp.sum(p, axis=1, keepdims=True)
    m_ref[h] = m_new
    return p, alpha


def _t5_bucket(dist):
    n = jnp.maximum(dist, 0)
    nf = jnp.maximum(n, 1).astype(F32)
    large = MAX_EXACT + (jnp.log(nf / MAX_EXACT) / math.log(MAX_DISTANCE / MAX_EXACT)
                         * (N_BUCKETS - MAX_EXACT)).astype(I32)
    large = jnp.minimum(large, N_BUCKETS - 1)
    return jnp.where(n < MAX_EXACT, n, large)


def _bias_lookup(table, dist):
    onehot = jax.nn.one_hot(_t5_bucket(dist), N_BUCKETS, dtype=F32)
    return jnp.dot(onehot, table.astype(F32), precision=lax.Precision.HIGHEST)


def _prompt_bias_tiles(table, t):
    assert t >= MAX_DISTANCE
    r = jnp.arange(t, dtype=I32)[:, None]
    c = jnp.arange(t, dtype=I32)[None, :]
    tiles = []
    for delta in range(3):
        dist = delta * t + r - c
        b = jnp.transpose(_bias_lookup(table, dist), (2, 0, 1))
        tiles.append(jnp.where(dist[None] >= 0, b, NEG))
    return jnp.stack(tiles)


def _dsa_prompt_kernel(k_sel, n_bits, qa_ref, qi_ref, wi_ref, k2_ref, kv_ref, bt_ref, o_ref,
                       qah_ref, qih_ref, key_ref, m_ref, l_ref, acc_ref):
    t = qa_ref.shape[0]
    i = pl.program_id(1)
    nj = i + 1
    row = lax.broadcasted_iota(I32, (t, t), 0)
    col = lax.broadcasted_iota(I32, (t, t), 1)
    lo = lax.broadcasted_iota(I32, (t, LANES), 1) < HEAD_DIM

    _split_halves(qa_ref, qah_ref, N_HEADS_A // 2)
    _split_halves(qi_ref, qih_ref, N_IDX_HEADS // 2)

    def score_chunk(j, carry):
        k0 = pl.multiple_of(j * t, t)
        kc = k2_ref[pl.ds(k0, t), :]
        wi = wi_ref[...]
        acc = jnp.zeros((t, t), F32)
        for h in range(N_IDX_HEADS):
            acc = acc + jnp.maximum(_dot_nt(qih_ref[h], kc), 0.0) * wi[:, h:h + 1]
        key = _sortable_key(acc)
        key_ref[j] = jnp.where(k0 + col <= i * t + row, key, jnp.int32(INT_MIN))
        return carry

    lax.fori_loop(0, nj, score_chunk, 0)

    strips = [slice(r0, r0 + SEARCH_ROWS) for r0 in range(0, t, SEARCH_ROWS)]
    scol = lax.broadcasted_iota(I32, (SEARCH_ROWS, t), 1)

    def widen(x):
        return jnp.concatenate([x] * (t // LANES), axis=1)

    def count_where(pred_fns):
        partial_counts = []
        for rows, pred_fn in zip(strips, pred_fns):
            def body(j, c, rows=rows, pred_fn=pred_fn):
                return c + _fold_lanes(jnp.where(pred_fn(key_ref[j, rows, :], j), 1, 0))
            partial_counts.append(lax.fori_loop(0, nj, body, jnp.zeros((SEARCH_ROWS, LANES), I32)))
        return [jnp.broadcast_to(jnp.sum(c.astype(F32), axis=1, keepdims=True), (SEARCH_ROWS, LANES))
                for c in partial_counts]

    def bit_step(b, tus):
        bit = lax.shift_left(jnp.int32(1), 31 - b)
        cand_us = [tu | bit for tu in tus]
        cands = [widen(cu ^ jnp.int32(INT_MIN)) for cu in cand_us]
        cnts = count_where([lambda kc, j, c=c: kc >= c for c in cands])
        return tuple(jnp.where(cnt >= k_sel, cu, tu) for cnt, cu, tu in zip(cnts, cand_us, tus))

    tus = lax.fori_loop(0, 32, bit_step, tuple(jnp.zeros((SEARCH_ROWS, LANES), I32) for _ in strips))
    thr_ns = [tu ^ jnp.int32(INT_MIN) for tu in tus]
    thr_ws = [widen(x) for x in thr_ns]

    n_gts = count_where([lambda kc, j, th=th: kc > th for th in thr_ws])
    n_eqs = count_where([lambda kc, j, th=th: kc == th for th in thr_ws])
    needs = [k_sel - n_gt for n_gt in n_gts]
    tied = [jnp.max(jnp.where((n_eq > need) & (thr_n != jnp.int32(INT_MIN)), 1.0, 0.0))
            for n_eq, need, thr_n in zip(n_eqs, needs, thr_ns)]

    def tie_search(_):
        def cut_step(b, c0s):
            bit = lax.shift_left(jnp.int32(1), n_bits - 1 - b)
            cands = [c0 | bit for c0 in c0s]
            cnts = count_where([lambda kc, j, th=th, cw=widen(c): (kc == th) & (j * t + scol < cw)
                                for th, c in zip(thr_ws, cands)])
            return tuple(jnp.where(cnt < need, c, c0) for cnt, need, c, c0 in zip(cnts, needs, cands, c0s))
        return lax.fori_loop(0, n_bits, cut_step, tuple(jnp.zeros((SEARCH_ROWS, LANES), I32) for _ in strips))

    cuts = lax.cond(functools.reduce(jnp.maximum, tied) > 0.0, tie_search,
                    lambda _: tuple(jnp.full((SEARCH_ROWS, LANES), 2 ** 30, I32) for _ in strips), 0)
    thr = widen(jnp.concatenate(thr_ns, axis=0))
    cut = widen(jnp.concatenate(cuts, axis=0))

    m_ref[...] = jnp.full(m_ref.shape, NEG, F32)
    l_ref[...] = jnp.zeros(l_ref.shape, F32)
    acc_ref[...] = jnp.zeros(acc_ref.shape, F32)

    def attn_chunk(j, carry):
        k0 = pl.multiple_of(j * t, t)
        kc = key_ref[j]
        sel = (kc > thr) | ((kc == thr) & (k0 + col <= cut))
        dsel = jnp.minimum(i - j, 2)
        for p in range(N_HEADS_A // 2):
            kp = kv_ref[pl.ds(k0, t), p * LANES:(p + 1) * LANES]
            vp = kv_ref[pl.ds(k0, t), A_WIDTH + p * LANES:A_WIDTH + (p + 1) * LANES]
            pv, al = [], []
            for half in range(2):
                h = 2 * p + half
                s = jnp.where(sel, _dot_nt(qah_ref[h], kp) + bt_ref[dsel, h], NEG)
                pr, alpha = _online_softmax_step(s, h, m_ref, l_ref)
                pv.append(_dot(pr.astype(BF16), vp))
                al.append(alpha)
            acc_ref[p] = acc_ref[p] * jnp.where(lo, al[0], al[1]) + jnp.where(lo, pv[0], pv[1])
        return carry

    lax.fori_loop(0, nj, attn_chunk, 0)

    for p in range(N_HEADS_A // 2):
        linv = jnp.where(lo, 1.0 / l_ref[2 * p], 1.0 / l_ref[2 * p + 1])
        o_ref[:, p * LANES:(p + 1) * LANES] = (acc_ref[p] * linv).astype(BF16)


def _dsa_prompt(qa, qi, wi, ki16, kva16, bias_tiles, batch, seq):
    t = min(ATTN_TILE, seq)
    nq = seq // t
    k_sel = min(TOPK_MAX, seq // 4)
    n_bits = max(1, (seq - 1).bit_length())
    kern = functools.partial(_dsa_prompt_kernel, k_sel, n_bits)
    return pl.pallas_call(
        kern,
        grid=(batch, nq),
        in_specs=[pl.BlockSpec((t, A_WIDTH), lambda b, i: (b * nq + i, 0)),
                  pl.BlockSpec((t, A_WIDTH), lambda b, i: (b * nq + i, 0)),
                  pl.BlockSpec((t, LANES), lambda b, i: (b * nq + i, 0)),
                  pl.BlockSpec((seq, LANES), lambda b, i: (b, 0)),
                  pl.BlockSpec((seq, 2 * A_WIDTH), lambda b, i: (b, 0)),
                  pl.BlockSpec(bias_tiles.shape, lambda b, i: (0, 0, 0, 0))],
        out_specs=pl.BlockSpec((t, A_WIDTH), lambda b, i: (b * nq + i, 0)),
        out_shape=jax.ShapeDtypeStruct((batch * seq, A_WIDTH), BF16),
        scratch_shapes=[pltpu.VMEM((N_HEADS_A, t, LANES), BF16),
                        pltpu.VMEM((N_IDX_HEADS, t, LANES), BF16),
                        pltpu.VMEM((nq, t, t), I32),
                        pltpu.VMEM((N_HEADS_A, t, LANES), F32),
                        pltpu.VMEM((N_HEADS_A, t, LANES), F32),
                        pltpu.VMEM((N_HEADS_A // 2, t, LANES), F32)],
        compiler_params=_cparams(("arbitrary", "arbitrary")),
        name="dsa_prompt",
    )(qa, qi, wi, ki16, kva16, bias_tiles)


def _subln(o0, o1, lam, lam_init, g):
    d = o0 - lam * o1
    return _rms(d, g) * (1.0 - lam_init)


def _diff_prompt_kernel(lam_init, lam_ref, qb_ref, kb_ref, vb_ref, bt_ref, g_ref, o_ref,
                        qh_ref, m_ref, l_ref, acc_ref):
    t = qb_ref.shape[0]
    i = pl.program_id(1)
    n_maps = 2 * N_HEADS_B
    _split_halves(qb_ref, qh_ref, N_HEADS_B)
    m_ref[...] = jnp.full(m_ref.shape, NEG, F32)
    l_ref[...] = jnp.zeros(l_ref.shape, F32)
    acc_ref[...] = jnp.zeros(acc_ref.shape, F32)

    def chunk(j, carry):
        k0 = pl.multiple_of(j * t, t)
        dsel = jnp.minimum(i - j, 2)
        for h in range(N_HEADS_B):
            kp = kb_ref[pl.ds(k0, t), h * LANES:(h + 1) * LANES]
            vp = vb_ref[pl.ds(k0, t), h * LANES:(h + 1) * LANES]
            for mp in range(2):
                c = 2 * h + mp
                s = _dot_nt(qh_ref[c], kp) + bt_ref[dsel, c]
                pr, alpha = _online_softmax_step(s, c, m_ref, l_ref)
                acc_ref[c] = acc_ref[c] * alpha + _dot(pr.astype(BF16), vp)
        return carry

    lax.fori_loop(0, i + 1, chunk, 0)

    lam = lam_ref[0]
    for h in range(N_HEADS_B):
        o0 = acc_ref[2 * h] / l_ref[2 * h]
        o1 = acc_ref[2 * h + 1] / l_ref[2 * h + 1]
        o_ref[:, h * LANES:(h + 1) * LANES] = _subln(o0, o1, lam, lam_init, g_ref[...]).astype(BF16)
    del n_maps


def _diff_prompt(lam, lam_init, qb, kb16, vb16, bias_tiles, g_sub, batch, seq):
    t = min(ATTN_TILE, seq)
    nq = seq // t
    n_maps = 2 * N_HEADS_B
    kern = functools.partial(_diff_prompt_kernel, lam_init)
    return pl.pallas_call(
        kern,
        grid=(batch, nq),
        in_specs=[pl.BlockSpec(memory_space=pltpu.SMEM),
                  pl.BlockSpec((t, B_WIDTH), lambda b, i: (b * nq + i, 0)),
                  pl.BlockSpec((seq, B_WIDTH), lambda b, i: (b, 0)),
                  pl.BlockSpec((seq, B_WIDTH), lambda b, i: (b, 0)),
                  pl.BlockSpec(bias_tiles.shape, lambda b, i: (0, 0, 0, 0)),
                  pl.BlockSpec((1, LANES), lambda b, i: (0, 0))],
        out_specs=pl.BlockSpec((t, B_WIDTH), lambda b, i: (b * nq + i, 0)),
        out_shape=jax.ShapeDtypeStruct((batch * seq, B_WIDTH), BF16),
        scratch_shapes=[pltpu.VMEM((n_maps, t, LANES), BF16),
                        pltpu.VMEM((n_maps, t, LANES), F32),
                        pltpu.VMEM((n_maps, t, LANES), F32),
                        pltpu.VMEM((n_maps, t, LANES), F32)],
        compiler_params=_cparams(("arbitrary", "arbitrary")),
        name="diff_prompt",
    )(lam, qb, kb16, vb16, bias_tiles, g_sub)


SAMPLE_ROWS = 8


def _sample_scores_kernel(n_pages, g_pages, dec_seq, pt_ref, *refs):
    page_refs = refs[:g_pages]
    qx_ref, wx_ref, knew_ref, key_ref = refs[g_pages:]
    del pt_ref
    step = pl.program_id(1)
    n_steps = n_pages // g_pages
    r8 = SAMPLE_ROWS

    def chunk_scores(kt16):
        d = jnp.maximum(_dot(qx_ref[0], kt16), 0.0) * wx_ref[0]
        acc = d[0:r8]
        for h in range(1, N_IDX_HEADS):
            acc = acc + d[h * r8:(h + 1) * r8]
        return _sortable_key(acc)

    for g in range(g_pages):
        key_ref[0, step * g_pages + g] = chunk_scores(page_refs[g][0].astype(BF16))

    @pl.when(step == n_steps - 1)
    def _():
        row = lax.broadcasted_iota(I32, (r8, LANES), 0)
        lane = lax.broadcasted_iota(I32, (r8, LANES), 1)
        key_new = chunk_scores(knew_ref[0])
        key_ref[0, n_pages] = jnp.where((lane <= row) & (lane < dec_seq), key_new, jnp.int32(INT_MIN))


def _sample_scores(page_table, cache_kidx_t, qx, wx, knew_t, dec_seq, g_pages):
    batch, n_pages = page_table.shape
    n_steps = n_pages // g_pages
    rows = N_IDX_HEADS * SAMPLE_ROWS
    kern = functools.partial(_sample_scores_kernel, n_pages, g_pages, dec_seq)
    page_specs = [pl.BlockSpec((1, D_IDX, LANES), functools.partial(
        lambda b, s, pt, g: (pt[b, s * g_pages + g], 0, 0), g=g)) for g in range(g_pages)]
    grid_spec = pltpu.PrefetchScalarGridSpec(
        num_scalar_prefetch=1,
        grid=(batch, n_steps),
        in_specs=page_specs + [pl.BlockSpec((1, rows, D_IDX), lambda b, s, pt: (b, 0, 0)),
                               pl.BlockSpec((1, rows, LANES), lambda b, s, pt: (b, 0, 0)),
                               pl.BlockSpec((1, D_IDX, LANES), lambda b, s, pt: (b, 0, 0))],
        out_specs=pl.BlockSpec((1, n_pages + 1, SAMPLE_ROWS, LANES), lambda b, s, pt: (b, 0, 0, 0)))
    return pl.pallas_call(
        kern,
        grid_spec=grid_spec,
        out_shape=jax.ShapeDtypeStruct((batch, n_pages + 1, SAMPLE_ROWS, LANES), I32),
        compiler_params=_cparams(("arbitrary", "arbitrary")),
        name="sample_scores",
    )(page_table, *([cache_kidx_t] * g_pages), qx, wx, knew_t)


def _sample_select_kernel(k_sel, n_bits, key_ref, mask_ref):
    shape = key_ref.shape
    colg = lax.broadcasted_iota(I32, shape, 1) * LANES + lax.broadcasted_iota(I32, shape, 3)

    def count(pred):
        c = jnp.sum(jnp.where(pred, 1.0, 0.0), axis=1, keepdims=True)
        return jnp.sum(c, axis=3, keepdims=True)

    stat = (shape[0], 1, shape[2], 1)

    def bit_step(b, tu):
        cand_u = tu | lax.shift_left(jnp.int32(1), 31 - b)
        cand = cand_u ^ jnp.int32(INT_MIN)
        return jnp.where(count(key_ref[...] >= cand) >= k_sel, cand_u, tu)

    tu = lax.fori_loop(0, 32, bit_step, jnp.zeros(stat, I32))
    thr = tu ^ jnp.int32(INT_MIN)
    need = k_sel - count(key_ref[...] > thr)

    def cut_step(b, c0):
        cand = c0 | lax.shift_left(jnp.int32(1), n_bits - 1 - b)
        return jnp.where(count((key_ref[...] == thr) & (colg < cand)) < need, cand, c0)

    cut = lax.fori_loop(0, n_bits, cut_step, jnp.zeros(stat, I32))
    keys = key_ref[...]
    sel = (keys > thr) | ((keys == thr) & (colg <= cut))
    sel = sel & (keys != jnp.int32(INT_MIN))
    mask_ref[...] = jnp.where(sel, 0.0, NEG)


def _sample_select(keys, k_sel):
    batch, n_chunks = keys.shape[:2]
    bt = 8 if batch % 8 == 0 else 1
    n_bits = max(1, (n_chunks * LANES - 1).bit_length())
    blk = (bt, n_chunks, SAMPLE_ROWS, LANES)
    return pl.pallas_call(
        functools.partial(_sample_select_kernel, k_sel, n_bits),
        grid=(batch // bt,),
        in_specs=[pl.BlockSpec(blk, lambda i: (i, 0, 0, 0))],
        out_specs=pl.BlockSpec(blk, lambda i: (i, 0, 0, 0)),
        out_shape=jax.ShapeDtypeStruct(keys.shape, F32),
        compiler_params=_cparams(("arbitrary",)),
        name="sample_select",
    )(keys)


def _sample_attn_kernel(lam_init, n_pages, g_pages, pt_ref, lam_ref, *refs):
    kva_refs = refs[:g_pages]
    kb_refs = refs[g_pages:2 * g_pages]
    vb_refs = refs[2 * g_pages:3 * g_pages]
    (qa_ref, qb_ref, mask_ref, maskn_ref, ba_ref, bb_ref, kvan_ref, kbn_ref, vbn_ref, g_ref,
     o_ref, m_ref, l_ref, acca_ref, accb_ref) = refs[3 * g_pages:]
    del pt_ref
    step = pl.program_id(1)
    n_steps = n_pages // g_pages
    r8 = SAMPLE_ROWS

    @pl.when(step == 0)
    def _():
        m_ref[...] = jnp.full(m_ref.shape, NEG, F32)
        l_ref[...] = jnp.zeros(l_ref.shape, F32)
        acca_ref[...] = jnp.zeros(acca_ref.shape, F32)
        accb_ref[...] = jnp.zeros(accb_ref.shape, F32)

    def both(kva, kb, vb, mask8, bsel):
        n = len(kva)
        cols = [slice(g * LANES, (g + 1) * LANES) for g in range(n)]
        s = jnp.concatenate(
            [_dot(qa_ref[0], kva[g][0, :A_WIDTH, :].astype(BF16))
             + (ba_ref[bsel[g]] + jnp.concatenate([mask8[g]] * N_HEADS_A, axis=0)) for g in range(n)], axis=1)
        pr, alpha = _online_softmax_step(s, 0, m_ref, l_ref)
        pr = pr.astype(BF16)
        pv = _dot_nt(pr[:, cols[0]], kva[0][0, A_WIDTH:, :].astype(BF16))
        for g in range(1, n):
            pv = pv + _dot_nt(pr[:, cols[g]], kva[g][0, A_WIDTH:, :].astype(BF16))
        acca_ref[...] = acca_ref[...] * jnp.concatenate([alpha] * (A_WIDTH // LANES), axis=1) + pv
        s = jnp.concatenate([_dot(qb_ref[0], kb[g][0].astype(BF16)) + bb_ref[bsel[g]] for g in range(n)], axis=1)
        pr, alpha = _online_softmax_step(s, 1, m_ref, l_ref)
        pr = pr.astype(BF16)
        pv = []
        for h in range(N_HEADS_B):
            rows = slice(2 * r8 * h, 2 * r8 * (h + 1))
            acc = None
            for g in range(n):
                d = _dot(pr[rows, cols[g]], vb[g][0, pl.ds(h, LANES, stride=N_HEADS_B), :].astype(BF16))
                acc = d if acc is None else acc + d
            pv.append(acc)
        accb_ref[...] = accb_ref[...] * alpha + jnp.concatenate(pv, axis=0)

    bsel = [0] * (g_pages - 1) + [jnp.where(step == n_steps - 1, 1, 0)]
    both(kva_refs, kb_refs, vb_refs, [mask_ref[0, g] for g in range(g_pages)], bsel)

    @pl.when(step == n_steps - 1)
    def _():
        both([kvan_ref], [kbn_ref], [vbn_ref], [maskn_ref[0, 0]], [2])
        lane = lax.broadcasted_iota(I32, (r8, A_WIDTH), 1)
        oa = jnp.zeros((r8, A_WIDTH), F32)
        la = jnp.concatenate([l_ref[0]] * (A_WIDTH // LANES), axis=1)
        acca = acca_ref[...] / la
        for h in range(N_HEADS_A):
            blk = acca[h * r8:(h + 1) * r8]
            oa = jnp.where((lane >= h * HEAD_DIM) & (lane < (h + 1) * HEAD_DIM), blk, oa)
        o_ref[0, :, :A_WIDTH] = oa.astype(BF16)
        lam = lam_ref[0]
        ob = accb_ref[...] / l_ref[1]
        for h in range(N_HEADS_B):
            r0 = 2 * h * r8
            o_ref[0, :, A_WIDTH + h * LANES:A_WIDTH + (h + 1) * LANES] = _subln(
                ob[r0:r0 + r8], ob[r0 + r8:r0 + 2 * r8], lam, lam_init, g_ref[...]).astype(BF16)


def _sample_attn(lam, lam_init, page_table, cache_kva_t, cache_kb_t, cache_vb, qa_x, qb_x, mask, bias_a, bias_b,
                 kva_new_t, kb_new_t, vb_new, g_sub, g_pages):
    batch, n_pages = page_table.shape
    n_steps = n_pages // g_pages
    rows = N_HEADS_A * SAMPLE_ROWS
    kern = functools.partial(_sample_attn_kernel, lam_init, n_pages, g_pages)

    def page_spec(width, g):
        return pl.BlockSpec((1, width, LANES), lambda b, s, pt: (pt[b, s * g_pages + g], 0, 0))

    in_specs = ([pl.BlockSpec(memory_space=pltpu.SMEM)]
                + [page_spec(2 * A_WIDTH, g) for g in range(g_pages)]
                + [page_spec(B_WIDTH, g) for g in range(g_pages)]
                + [page_spec(B_WIDTH, g) for g in range(g_pages)]
                + [pl.BlockSpec((1, rows, A_WIDTH), lambda b, s, pt: (b, 0, 0)),
                   pl.BlockSpec((1, rows, B_WIDTH), lambda b, s, pt: (b, 0, 0)),
                   pl.BlockSpec((1, g_pages, SAMPLE_ROWS, LANES), lambda b, s, pt: (b, s, 0, 0)),
                   pl.BlockSpec((1, 1, SAMPLE_ROWS, LANES), lambda b, s, pt: (b, n_pages, 0, 0)),
                   pl.BlockSpec(bias_a.shape, lambda b, s, pt: (0, 0, 0)),
                   pl.BlockSpec(bias_b.shape, lambda b, s, pt: (0, 0, 0)),
                   pl.BlockSpec((1, 2 * A_WIDTH, LANES), lambda b, s, pt: (b, 0, 0)),
                   pl.BlockSpec((1, B_WIDTH, LANES), lambda b, s, pt: (b, 0, 0)),
                   pl.BlockSpec((1, B_WIDTH, LANES), lambda b, s, pt: (b, 0, 0)),
                   pl.BlockSpec((1, LANES), lambda b, s, pt: (0, 0))])
    grid_spec = pltpu.PrefetchScalarGridSpec(
        num_scalar_prefetch=1,
        grid=(batch, n_steps),
        in_specs=in_specs,
        out_specs=pl.BlockSpec((1, SAMPLE_ROWS, A_WIDTH + B_WIDTH), lambda b, s, pt: (b, 0, 0)),
        scratch_shapes=[pltpu.VMEM((2, rows, LANES), F32),
                        pltpu.VMEM((2, rows, LANES), F32),
                        pltpu.VMEM((rows, A_WIDTH), F32),
                        pltpu.VMEM((rows, LANES), F32)])
    return pl.pallas_call(
        kern,
        grid_spec=grid_spec,
        out_shape=jax.ShapeDtypeStruct((batch, SAMPLE_ROWS, A_WIDTH + B_WIDTH), BF16),
        compiler_params=_cparams(("arbitrary", "arbitrary")),
        name="sample_attn",
    )(page_table, lam, *([cache_kva_t] * g_pages), *([cache_kb_t] * g_pages), *([cache_vb] * g_pages),
      qa_x, qb_x, mask, mask, bias_a, bias_b, kva_new_t, kb_new_t, vb_new, g_sub)


def _sample_bias(table, past, dec_seq, causal_new):
    t = jnp.arange(SAMPLE_ROWS, dtype=I32)[:, None]
    c = jnp.arange(LANES, dtype=I32)[None, :]
    far = jnp.full((SAMPLE_ROWS, LANES), MAX_DISTANCE, I32)
    last = t + LANES - c
    new = t - c
    out = []
    for dist in (far, last, new):
        b = jnp.transpose(_bias_lookup(table, dist), (2, 0, 1))
        out.append(b.reshape(table.shape[1] * SAMPLE_ROWS, LANES))
    if causal_new:
        ok = jnp.tile((new >= 0) & (c < dec_seq), (table.shape[1], 1))
        out[2] = jnp.where(ok, out[2], NEG)
    del past
    return jnp.stack(out)


def _block_diag_rows(q):
    b, t = q.shape[:2]
    q = jnp.pad(q, ((0, 0), (0, SAMPLE_ROWS - t), (0, 0), (0, 0)))
    q = jnp.transpose(q, (0, 2, 1, 3))
    eye = jnp.eye(8, dtype=q.dtype)
    return jnp.einsum('bctd,cg->bctgd', q, eye).reshape(b, 8 * SAMPLE_ROWS, 8 * HEAD_DIM)


def _tail1_kernel(n_mix, x_ref, *refs):
    o_refs = refs[:n_mix]
    w_refs = refs[n_mix:2 * n_mix]
    g_ref, wq_ref, x1_ref, qm_ref = refs[2 * n_mix:]
    x1 = x_ref[...]
    for o_ref, w_ref in zip(o_refs, w_refs):
        x1 = x1 + _dot(o_ref[...], w_ref[...])
    x1_ref[...] = x1
    hm = _rms(x1, g_ref[...]).astype(BF16)
    qm_ref[...] = (_dot(hm, wq_ref[...]) * HEAD_DIM ** -0.5).astype(BF16)


def _tail1(x, mixes, g_mem, wq16):
    n, d = x.shape
    tm = min(n, 512)
    n_mix = len(mixes)
    mw = wq16.shape[1]
    return pl.pallas_call(
        functools.partial(_tail1_kernel, n_mix),
        grid=(n // tm,),
        in_specs=([pl.BlockSpec((tm, d), lambda i: (i, 0))]
                  + [pl.BlockSpec((tm, o.shape[1]), lambda i: (i, 0)) for o, _ in mixes]
                  + [pl.BlockSpec(w.shape, lambda i: (0, 0)) for _, w in mixes]
                  + [pl.BlockSpec((1, d), lambda i: (0, 0)),
                     pl.BlockSpec(wq16.shape, lambda i: (0, 0))]),
        out_specs=[pl.BlockSpec((tm, d), lambda i: (i, 0)), pl.BlockSpec((tm, mw), lambda i: (i, 0))],
        out_shape=[jax.ShapeDtypeStruct((n, d), F32), jax.ShapeDtypeStruct((n, mw), BF16)],
        compiler_params=_cparams(("arbitrary",)),
        name="tail1",
    )(x, *[o for o, _ in mixes], *[w for _, w in mixes], g_mem, wq16)


def _tail2_kernel(qm_ref, mkv_ref, x1_ref, wo_ref, g_ref, wr_ref, br_ref, x2_ref, h_ref, route_ref):
    tq = qm_ref.shape[1]
    mem_w = N_HEADS_MEM * HEAD_DIM
    lo = lax.broadcasted_iota(I32, (tq, LANES), 1) < HEAD_DIM
    mkv = mkv_ref[0].astype(BF16)
    outs = []
    for p in range(N_HEADS_MEM // 2):
        qp = qm_ref[0, :, p * LANES:(p + 1) * LANES]
        kp = mkv[:, p * LANES:(p + 1) * LANES]
        vp = mkv[:, mem_w + p * LANES:mem_w + (p + 1) * LANES]
        pv = []
        for half in range(2):
            qh = jnp.where(lo == (half == 0), qp, jnp.zeros_like(qp))
            s = _dot_nt(qh, kp)
            e = jnp.exp(s - jnp.max(s, axis=1, keepdims=True))
            pv.append(_dot(e.astype(BF16), vp) / jnp.sum(e, axis=1, keepdims=True))
        outs.append(jnp.where(lo, pv[0], pv[1]))
    om = jnp.concatenate(outs, axis=1).astype(BF16)
    x2 = x1_ref[0] + _dot(om, wo_ref[...])
    x2_ref[0] = x2
    h = _rms(x2, g_ref[...])
    h_ref[0] = h.astype(BF16)

    lg = jnp.dot(h, wr_ref[...], preferred_element_type=F32, precision=lax.Precision.HIGHEST) + br_ref[...]
    lane = lax.broadcasted_iota(I32, (tq, LANES), 1)
    lane_f = lane.astype(F32)
    big = jnp.float32(1e9)
    is_g = lane < N_GROUPS
    m1 = jnp.max(jnp.where(is_g, lg, -jnp.inf), axis=1, keepdims=True)
    grp = jnp.min(jnp.where(is_g & (lg == m1), lane_f, big), axis=1, keepdims=True)
    p_grp = 1.0 / jnp.sum(jnp.where(is_g, jnp.exp(lg - m1), 0.0), axis=1, keepdims=True)
    e0 = N_GROUPS + grp * EXPERTS_PER_GROUP
    in_g = (lane_f >= e0) & (lane_f < e0 + EXPERTS_PER_GROUP)
    v1 = jnp.max(jnp.where(in_g, lg, -jnp.inf), axis=1, keepdims=True)
    i1 = jnp.min(jnp.where(in_g & (lg == v1), lane_f, big), axis=1, keepdims=True)
    rest = in_g & (lane_f != i1)
    v2 = jnp.max(jnp.where(rest, lg, -jnp.inf), axis=1, keepdims=True)
    i2 = jnp.min(jnp.where(rest & (lg == v2), lane_f, big), axis=1, keepdims=True)
    e2 = jnp.exp(v2 - v1)
    den = 1.0 + e2
    g1 = (1.0 / den) * p_grp
    g2 = (e2 / den) * p_grp
    route = jnp.where(lane == 0, i1 - N_GROUPS, 0.0)
    route = jnp.where(lane == 1, i2 - N_GROUPS, route)
    route = jnp.where(lane == 2, g1, route)
    route = jnp.where(lane == 3, g2, route)
    route_ref[0] = route


def _tail2(qm, mkv, x1, wo16, g_ffn, wr, br):
    b, t, d = x1.shape
    tq = min(t, 512)
    mem_w = qm.shape[2]
    n_mem = mkv.shape[1]
    return pl.pallas_call(
        _tail2_kernel,
        grid=(b, t // tq),
        in_specs=[pl.BlockSpec((1, tq, mem_w), lambda i, j: (i, j, 0)),
                  pl.BlockSpec((1, n_mem, 2 * mem_w), lambda i, j: (i, 0, 0)),
                  pl.BlockSpec((1, tq, d), lambda i, j: (i, j, 0)),
                  pl.BlockSpec(wo16.shape, lambda i, j: (0, 0)),
                  pl.BlockSpec((1, d), lambda i, j: (0, 0)),
                  pl.BlockSpec(wr.shape, lambda i, j: (0, 0)),
                  pl.BlockSpec((1, LANES), lambda i, j: (0, 0))],
        out_specs=[pl.BlockSpec((1, tq, d), lambda i, j: (i, j, 0)),
                   pl.BlockSpec((1, tq, d), lambda i, j: (i, j, 0)),
                   pl.BlockSpec((1, tq, LANES), lambda i, j: (i, j, 0))],
        out_shape=[jax.ShapeDtypeStruct((b, t, d), F32), jax.ShapeDtypeStruct((b, t, d), BF16),
                   jax.ShapeDtypeStruct((b, t, LANES), F32)],
        compiler_params=_cparams(("arbitrary", "arbitrary")),
        name="tail2",
    )(qm, mkv, x1, wo16, g_ffn, wr, br)


FLAG_FIRST, FLAG_LAST, FLAG_VALID = 1, 2, 4


def _moe_ffn_kernel(pb_ref, pc_ref, fl_ref, be_ref, h_ref, rt_ref, gate_ref, wgu_ref, wdn_ref, y_ref, acc_ref):
    del pb_ref, be_ref
    p = pl.program_id(0)
    fl = fl_ref[p]
    c = h_ref.shape[0]
    r = acc_ref.shape[0]

    @pl.when((fl & FLAG_FIRST) != 0)
    def _():
        acc_ref[...] = jnp.zeros(acc_ref.shape, F32)

    @pl.when((fl & FLAG_VALID) != 0)
    def _():
        rel = rt_ref[...] - pc_ref[p] * c
        onehot = jnp.where(rel == lax.broadcasted_iota(I32, (r, c), 1), 1.0, 0.0).astype(BF16)
        acc_ref[...] += _dot(onehot, h_ref[...])

    @pl.when((fl & FLAG_LAST) != 0)
    def _():
        gu = _dot(acc_ref[...].astype(BF16), wgu_ref[0])
        gate = gu[:, :D_EXPERT]
        up = gu[:, D_EXPERT:]
        act = gate * (1.0 / (1.0 + jnp.exp(-gate))) * up
        y_ref[...] = (_dot(act.astype(BF16), wdn_ref[0]) * gate_ref[...]).astype(BF16)


def _moe_ffn(pb, pc, flags, block_expert, h16, row_tok, row_gate, wgu16, wdn16, chunk):
    n_rows = row_tok.shape[0]
    d = h16.shape[1]
    r = EXPERT_ROWS
    grid_spec = pltpu.PrefetchScalarGridSpec(
        num_scalar_prefetch=4,
        grid=(pb.shape[0],),
        in_specs=[pl.BlockSpec((chunk, d), lambda p, pb, pc, fl, be: (pc[p], 0)),
                  pl.BlockSpec((r, 1), lambda p, pb, pc, fl, be: (pb[p], 0)),
                  pl.BlockSpec((r, 1), lambda p, pb, pc, fl, be: (pb[p], 0)),
                  pl.BlockSpec((1,) + wgu16.shape[1:], lambda p, pb, pc, fl, be: (be[pb[p]], 0, 0)),
                  pl.BlockSpec((1,) + wdn16.shape[1:], lambda p, pb, pc, fl, be: (be[pb[p]], 0, 0))],
        out_specs=pl.BlockSpec((r, d), lambda p, pb, pc, fl, be: (pb[p], 0)),
        scratch_shapes=[pltpu.VMEM((r, d), F32)])
    return pl.pallas_call(
        _moe_ffn_kernel,
        grid_spec=grid_spec,
        out_shape=jax.ShapeDtypeStruct((n_rows, d), BF16),
        compiler_params=_cparams(("arbitrary",)),
        name="moe_ffn",
    )(pb, pc, flags, block_expert, h16, row_tok.reshape(n_rows, 1), row_gate.reshape(n_rows, 1), wgu16, wdn16)


def _moe_combine_kernel(pb_ref, pc_ref, fl_ref, y_ref, rt_ref, x2_ref, g_ref, o_ref, acc_ref):
    del pb_ref
    p = pl.program_id(0)
    fl = fl_ref[p]
    c = acc_ref.shape[0]
    r = y_ref.shape[0]

    @pl.when((fl & FLAG_FIRST) != 0)
    def _():
        acc_ref[...] = jnp.zeros(acc_ref.shape, F32)

    @pl.when((fl & FLAG_VALID) != 0)
    def _():
        rel = rt_ref[0] - pc_ref[p] * c
        onehot_t = jnp.where(rel == lax.broadcasted_iota(I32, (c, r), 0), 1.0, 0.0).astype(BF16)
        acc_ref[...] += _dot(onehot_t, y_ref[...])

    @pl.when((fl & FLAG_LAST) != 0)
    def _():
        o_ref[...] = _rms(x2_ref[...] + acc_ref[...], g_ref[...])


def _moe_combine(pb, pc, flags, y_rows, row_tok, x2, g_final, chunk):
    n, d = x2.shape
    r = EXPERT_ROWS
    n_blocks = y_rows.shape[0] // r
    grid_spec = pltpu.PrefetchScalarGridSpec(
        num_scalar_prefetch=3,
        grid=(pb.shape[0],),
        in_specs=[pl.BlockSpec((r, d), lambda p, pb, pc, fl: (pb[p], 0)),
                  pl.BlockSpec((1, 1, r), lambda p, pb, pc, fl: (pb[p], 0, 0)),
                  pl.BlockSpec((chunk, d), lambda p, pb, pc, fl: (pc[p], 0)),
                  pl.BlockSpec((1, d), lambda p, pb, pc, fl: (0, 0))],
        out_specs=pl.BlockSpec((chunk, d), lambda p, pb, pc, fl: (pc[p], 0)),
        scratch_shapes=[pltpu.VMEM((chunk, d), F32)])
    return pl.pallas_call(
        _moe_combine_kernel,
        grid_spec=grid_spec,
        out_shape=jax.ShapeDtypeStruct((n, d), F32),
        compiler_params=_cparams(("arbitrary",)),
        name="moe_combine",
    )(pb, pc, flags, y_rows, row_tok.reshape(n_blocks, 1, r), x2, g_final)


def _moe_and_final(x2, h16, route, wgu16, wdn16, g_final):
    n, d = x2.shape
    n_assign = 2 * n
    experts = route[:, :2].astype(I32).reshape(n_assign)
    onehot = (experts[:, None] == jnp.arange(N_EXPERTS, dtype=I32)[None, :]).astype(I32)
    running = jnp.cumsum(onehot, axis=0)
    rank = jnp.sum((running - onehot) * onehot, axis=1)
    counts = running[-1]
    padded = (counts + EXPERT_ROWS - 1) // EXPERT_ROWS * EXPERT_ROWS
    pad_ends = jnp.cumsum(padded)
    pad_starts = pad_ends - padded
    pos = jnp.sum(onehot * pad_starts[None, :], axis=1) + rank
    n_blocks = -(-n_assign // EXPERT_ROWS) + N_EXPERTS
    n_rows = n_blocks * EXPERT_ROWS
    tok = jnp.arange(n_assign, dtype=I32) // 2
    gate_bits = lax.bitcast_convert_type(route[:, 2:4].reshape(n_assign), I32)
    rows = jnp.concatenate([jnp.full((n_rows, 1), -1, I32), jnp.zeros((n_rows, 1), I32)], axis=1)
    rows = rows.at[pos].set(jnp.stack([tok, gate_bits], axis=1))
    row_tok = rows[:, 0]
    row_gate = lax.bitcast_convert_type(rows[:, 1], F32)
    block_start = jnp.arange(n_blocks, dtype=I32) * EXPERT_ROWS
    block_expert = jnp.minimum(jnp.sum((pad_ends[None, :] <= block_start[:, None]).astype(I32), axis=1),
                               N_EXPERTS - 1).astype(I32)

    chunk = min(MOE_CHUNK, n)
    assert n % chunk == 0
    n_chunks = n // chunk
    rt2 = row_tok.reshape(n_blocks, EXPERT_ROWS)
    hi = jnp.max(rt2, axis=1)
    lo = jnp.min(jnp.where(rt2 >= 0, rt2, n), axis=1)
    lo_c = jnp.where(hi < 0, 0, lo // chunk)
    hi_c = jnp.where(hi < 0, 0, hi // chunk)
    n_ch = hi_c - lo_c + 1
    ends = jnp.cumsum(n_ch)
    starts = ends - n_ch
    total = ends[-1]
    n_pairs = N_EXPERTS * n_chunks + n_blocks
    p = jnp.arange(n_pairs, dtype=I32)
    valid = p < total
    pb = jnp.minimum(jnp.sum((ends[None, :] <= p[:, None]).astype(I32), axis=1), n_blocks - 1).astype(I32)
    pc = jnp.where(valid, lo_c[pb] + p - starts[pb], hi_c[n_blocks - 1]).astype(I32)
    flags = (jnp.where(valid & (p == starts[pb]), FLAG_FIRST, 0) + jnp.where(valid & (p == ends[pb] - 1), FLAG_LAST, 0)
             + jnp.where(valid, FLAG_VALID, 0)).astype(I32)
    y_rows = _moe_ffn(pb, pc, flags, block_expert, h16, row_tok, row_gate, wgu16, wdn16, chunk)

    order = jnp.argsort(jnp.where(valid, pc * n_blocks + pb, jnp.int32(2 ** 30)))
    pb2, pc2 = pb[order], pc[order]
    last_valid = jnp.maximum(total - 1, 0)
    pb2 = jnp.where(valid, pb2, pb2[last_valid])
    pc2 = jnp.where(valid, pc2, pc2[last_valid])
    prev_c = jnp.concatenate([jnp.full((1,), -1, I32), pc2[:-1]])
    next_c = jnp.concatenate([pc2[1:], jnp.full((1,), -1, I32)])
    flags2 = (jnp.where(valid & (pc2 != prev_c), FLAG_FIRST, 0)
              + jnp.where(valid & ((pc2 != next_c) | (p == total - 1)), FLAG_LAST, 0)
              + jnp.where(valid, FLAG_VALID, 0)).astype(I32)
    return _moe_combine(pb2, pc2, flags2, y_rows, row_tok, x2, g_final, chunk)


def kernel(x_prompt, x_sample, cache_kv_a, cache_k_idx, cache_k_b, cache_v_b, cache_mem_kv, page_table, mem_prompt, rel_bias, g_mix, w_in, g_kidx, lambda_q1, lambda_k1, lambda_q2, lambda_k2, g_subln, w_out, g_mem, g_memin, w_mem_q, w_mem_kv, w_mem_o, g_ffn, w_router1, b_router1, w_router2, b_router2, w_gate_up, w_down, g_final):
    depth = g_mix.shape[0]
    assert depth == 1
    layer = 0
    batch, seq, d = x_prompt.shape
    dec_batch, dec_seq, _ = x_sample.shape
    n_pages = page_table.shape[1]
    n_phys = cache_kv_a.shape[1]
    past = n_pages * LANES
    n_mem = mem_prompt.shape[1]
    mem_w = N_HEADS_MEM * HEAD_DIM
    assert cache_kv_a.shape[2] == LANES and dec_seq <= SAMPLE_ROWS

    table_a = rel_bias[:, :N_HEADS_A]
    table_b = rel_bias[:, N_HEADS_A:]
    lam_init = 0.8 - 0.6 * math.exp(-0.3 * layer)
    lam = (jnp.exp(jnp.sum(lambda_q1[layer] * lambda_k1[layer]).astype(F32))
           - jnp.exp(jnp.sum(lambda_q2[layer] * lambda_k2[layer]).astype(F32)) + lam_init).reshape(1)

    w = w_in[layer]
    offs = [0, 512, 1024, 1536, 2048, 2112, 2120, 2632, 3144, 3656]
    q_a, k_a, v_a, q_i, k_i, w_i, q_b, k_b, v_b = [w[:, offs[s]:offs[s + 1]] for s in range(9)]
    w_packed = jnp.concatenate(
        [q_a, k_a, v_a, q_i, k_i, k_i, jnp.pad(w_i, ((0, 0), (0, LANES - N_IDX_HEADS))), q_b, k_b, v_b],
        axis=1).astype(BF16)
    gk2 = jnp.concatenate([g_kidx[layer], g_kidx[layer]]).reshape(1, LANES)
    g_mix_l = g_mix[layer].reshape(1, d)
    w_out16 = w_out[layer].astype(BF16)
    wq16 = w_mem_q[layer].astype(BF16)
    wo16 = w_mem_o[layer].astype(BF16)
    wr = jnp.concatenate([w_router1[layer],
                          jnp.transpose(w_router2[layer], (1, 0, 2)).reshape(d, N_EXPERTS),
                          jnp.zeros((d, LANES - N_GROUPS - N_EXPERTS), F32)], axis=1)
    br = jnp.concatenate([b_router1[layer], b_router2[layer].reshape(N_EXPERTS),
                          jnp.zeros((LANES - N_GROUPS - N_EXPERTS,), F32)]).reshape(1, LANES)
    wgu16 = w_gate_up[layer].astype(BF16)
    wdn16 = w_down[layer].astype(BF16)
    g_sub = g_subln[layer].reshape(1, LANES)
    g_mem_l = g_mem[layer].reshape(1, d)
    g_ffn_l = g_ffn[layer].reshape(1, d)
    g_fin = g_final.reshape(1, d)

    n_p = batch * seq
    (qa, kva, kva16, qi, ki, ki16, wi, qb, kb, vb, kb16, vb16) = _proj(
        x_prompt.reshape(n_p, d), g_mix_l, w_packed, gk2)
    t_attn = min(ATTN_TILE, seq)
    o_a = _dsa_prompt(qa, qi, wi, ki16, kva16, _prompt_bias_tiles(table_a, t_attn), batch, seq)
    o_b = _diff_prompt(lam, lam_init, qb, kb16, vb16, _prompt_bias_tiles(table_b, t_attn), g_sub, batch, seq)
    mkv_p = _norm_matmul(mem_prompt.reshape(batch * n_mem, d), g_memin[layer].reshape(1, d),
                         w_mem_kv[layer].astype(BF16))
    x1, qm = _tail1(x_prompt.reshape(n_p, d), [(o_a, w_out16[:A_WIDTH]), (o_b, w_out16[A_WIDTH:])],
                    g_mem_l, wq16)
    x2, h_ffn, route = _tail2(qm.reshape(batch, seq, mem_w), mkv_p.reshape(batch, n_mem, 2 * mem_w),
                              x1.reshape(batch, seq, d), wo16, g_ffn_l, wr, br)
    y_prompt = _moe_and_final(x2.reshape(n_p, d), h_ffn.reshape(n_p, d), route.reshape(n_p, LANES),
                              wgu16, wdn16, g_fin).reshape(batch, seq, d)

    n_s = dec_batch * dec_seq
    (qa_s, kva_s, kva16_s, qi_s, ki_s, ki16_s, wi_s, qb_s, kb_s, vb_s, kb16_s, vb16_s) = _proj(
        x_sample.reshape(n_s, d), g_mix_l, w_packed, gk2)
    pad_t = SAMPLE_ROWS - dec_seq
    qx = jnp.transpose(jnp.pad(qi_s.reshape(dec_batch, dec_seq, N_IDX_HEADS, D_IDX),
                               ((0, 0), (0, pad_t), (0, 0), (0, 0))), (0, 2, 1, 3)
                       ).reshape(dec_batch, N_IDX_HEADS * SAMPLE_ROWS, D_IDX)
    wx = jnp.transpose(jnp.pad(wi_s[:, :N_IDX_HEADS].reshape(dec_batch, dec_seq, N_IDX_HEADS),
                               ((0, 0), (0, pad_t), (0, 0))), (0, 2, 1)
                       ).reshape(dec_batch, N_IDX_HEADS * SAMPLE_ROWS, 1)
    wx = jnp.broadcast_to(wx, (dec_batch, N_IDX_HEADS * SAMPLE_ROWS, LANES))

    def new_chunk_t(a):
        a = jnp.pad(a.reshape(dec_batch, dec_seq, a.shape[1]), ((0, 0), (0, LANES - dec_seq), (0, 0)))
        return jnp.transpose(a, (0, 2, 1))

    kidx_t = jnp.transpose(cache_k_idx[layer], (0, 2, 1))
    kva_t = jnp.transpose(cache_kv_a[layer], (0, 2, 3, 4, 1)).reshape(n_phys, 2 * A_WIDTH, LANES)
    kb_t = jnp.transpose(cache_k_b[layer], (0, 2, 3, 4, 1)).reshape(n_phys, B_WIDTH, LANES)
    vb_r = cache_v_b[layer].reshape(n_phys, LANES * N_HEADS_B, 2 * HEAD_DIM)
    vb_new = jnp.pad(vb_s.reshape(dec_batch, dec_seq, N_HEADS_B, 2 * HEAD_DIM),
                     ((0, 0), (0, LANES - dec_seq), (0, 0), (0, 0))).reshape(dec_batch, LANES * N_HEADS_B, 2 * HEAD_DIM)

    g_idx = 16 if n_pages % 16 == 0 else 1
    keys = _sample_scores(page_table, kidx_t, qx, wx, new_chunk_t(ki16_s[:, :D_IDX]), dec_seq, g_idx)
    mask = _sample_select(keys, min(TOPK_MAX, (past + dec_seq) // 4))
    g_att = 8 if n_pages % 8 == 0 else 1
    o_s = _sample_attn(
        lam, lam_init, page_table, kva_t, kb_t, vb_r,
        _block_diag_rows(qa_s.reshape(dec_batch, dec_seq, N_HEADS_A, HEAD_DIM)),
        _block_diag_rows(qb_s.reshape(dec_batch, dec_seq, 2 * N_HEADS_B, HEAD_DIM)),
        mask, _sample_bias(table_a, past, dec_seq, False), _sample_bias(table_b, past, dec_seq, True),
        new_chunk_t(kva16_s), new_chunk_t(kb16_s), vb_new, g_sub, g_att)
    o_s = o_s[:, :dec_seq].reshape(n_s, A_WIDTH + B_WIDTH)
    x1_s, qm_s = _tail1(x_sample.reshape(n_s, d), [(o_s, w_out16)], g_mem_l, wq16)
    x2_s, h_s, route_s = _tail2(qm_s.reshape(dec_batch, dec_seq, mem_w),
                                cache_mem_kv[layer].reshape(dec_batch, n_mem, 2 * mem_w),
                                x1_s.reshape(dec_batch, dec_seq, d), wo16, g_ffn_l, wr, br)
    y_sample = _moe_and_final(x2_s.reshape(n_s, d), h_s.reshape(n_s, d), route_s.reshape(n_s, LANES),
                              wgu16, wdn16, g_fin).reshape(dec_batch, dec_seq, d)

    return (y_prompt, y_sample,
            kva.reshape(1, batch, seq, 2, N_HEADS_A, HEAD_DIM),
            ki[:, :D_IDX].reshape(1, batch, seq, D_IDX),
            kb.reshape(1, batch, seq, N_HEADS_B, 2, HEAD_DIM),
            vb.reshape(1, batch, seq, N_HEADS_B, 2 * HEAD_DIM),
            mkv_p.reshape(1, batch, n_mem, 2, N_HEADS_MEM, HEAD_DIM),
            kva_s.reshape(1, dec_batch, dec_seq, 2, N_HEADS_A, HEAD_DIM),
            ki_s[:, :D_IDX].reshape(1, dec_batch, dec_seq, D_IDX),
            kb_s.reshape(1, dec_batch, dec_seq, N_HEADS_B, 2, HEAD_DIM),
            vb_s.reshape(1, dec_batch, dec_seq, N_HEADS_B, 2 * HEAD_DIM))
```

```python
import functools
import math

import jax
import jax.numpy as jnp
from jax import lax
from jax.experimental import pallas as pl
from jax.experimental.pallas import tpu as pltpu

F32 = jnp.float32
BF16 = jnp.bfloat16
I32 = jnp.int32

HEAD_DIM = 64
N_HEADS_A = 8
N_HEADS_B = 4
A_WIDTH = 512
B_WIDTH = 512
N_IDX_HEADS = 8
D_IDX = 64
TOPK_MAX = 256
N_HEADS_MEM = 4
N_BUCKETS = 32
MAX_EXACT = 16
MAX_DISTANCE = 128
N_GROUPS = 4
EXPERTS_PER_GROUP = 8
N_EXPERTS = 32
D_EXPERT = 512
NEG = -1e30
EPS = 1e-6
LANES = 128
INT_MIN = -(2 ** 31)

SEG_QA, SEG_KVA, SEG_QI, SEG_KI, SEG_WI, SEG_QB, SEG_KB, SEG_VB = (
    (0, 512), (512, 1536), (1536, 2048), (2048, 2176), (2176, 2304), (2304, 2816), (2816, 3328), (3328, 3840))
PACKED_COLS = 3840

VMEM_LIMIT = 56 * 1024 * 1024
ATTN_TILE = 256
KEY_SUB = 128
EXPERT_ROWS = 256
MOE_CHUNK = 1024


def _cparams(sem):
    return pltpu.CompilerParams(dimension_semantics=sem, vmem_limit_bytes=VMEM_LIMIT)


def _dot(a, b):
    return jnp.dot(a, b, preferred_element_type=F32)


def _dot_nt(a, b):
    return lax.dot_general(a, b, (((1,), (1,)), ((), ())), preferred_element_type=F32)


def _rms(x, g):
    return x * lax.rsqrt(jnp.mean(x * x, axis=-1, keepdims=True) + EPS) * g


def _sortable_key(score):
    bits = lax.bitcast_convert_type(score, I32)
    key = jnp.where(bits < 0, bits ^ jnp.int32(0x7FFFFFFF), bits)
    return jnp.where(bits == jnp.int32(INT_MIN), 0, key)


def _proj_kernel(x_ref, g_ref, w_ref, gk_ref, qa_ref, kva_ref, ka16_ref, vat_ref, qi_ref, ki_ref, ki16_ref, wi_ref,
                 qb_ref, kb_ref, vb_ref, kb16_ref, vbt_ref):
    h = _rms(x_ref[...], g_ref[...]).astype(BF16)
    tt = vat_ref.shape[2]

    def seg(s):
        return _dot(h, w_ref[:, s[0]:s[1]])

    def store_transposed(dst_ref, v):
        for c in range(dst_ref.shape[0]):
            dst_ref[c] = v[c * tt:(c + 1) * tt, :].T.astype(BF16)

    qa_ref[...] = (seg(SEG_QA) * HEAD_DIM ** -0.5).astype(BF16)
    kva = seg(SEG_KVA)
    kva_ref[...] = kva
    ka16_ref[...] = kva[:, :A_WIDTH].astype(BF16)
    store_transposed(vat_ref, kva[:, A_WIDTH:])
    qi_ref[...] = (seg(SEG_QI) * D_IDX ** -0.5).astype(BF16)
    ki = _rms(seg(SEG_KI), gk_ref[...])
    ki_ref[...] = ki
    ki16_ref[...] = ki.astype(BF16)
    wi_ref[...] = seg(SEG_WI) * N_IDX_HEADS ** -0.5
    qb_ref[...] = (seg(SEG_QB) * HEAD_DIM ** -0.5).astype(BF16)
    kb = seg(SEG_KB)
    kb_ref[...] = kb
    kb16_ref[...] = kb.astype(BF16)
    vb = seg(SEG_VB)
    vb_ref[...] = vb
    store_transposed(vbt_ref, vb)


def _proj(x, g, w_packed, gk2):
    n, d = x.shape
    tm = min(n, 512)
    tt = min(ATTN_TILE, tm)
    vt = (A_WIDTH, None)
    widths = [(512, BF16), (1024, F32), (512, BF16), vt, (512, BF16), (128, F32), (128, BF16), (128, F32),
              (512, BF16), (512, F32), (512, F32), (512, BF16), vt]

    def spec(w, dt):
        if dt is None:
            return pl.BlockSpec((tm // tt, w, tt), lambda i: (i, 0, 0))
        return pl.BlockSpec((tm, w), lambda i: (i, 0))

    def shape(w, dt):
        if dt is None:
            return jax.ShapeDtypeStruct((n // tt, w, tt), BF16)
        return jax.ShapeDtypeStruct((n, w), dt)

    return pl.pallas_call(
        _proj_kernel,
        grid=(n // tm,),
        in_specs=[pl.BlockSpec((tm, d), lambda i: (i, 0)),
                  pl.BlockSpec((1, d), lambda i: (0, 0)),
                  pl.BlockSpec((d, PACKED_COLS), lambda i: (0, 0)),
                  pl.BlockSpec((1, LANES), lambda i: (0, 0))],
        out_specs=[spec(w, dt) for w, dt in widths],
        out_shape=[shape(w, dt) for w, dt in widths],
        compiler_params=_cparams(("arbitrary",)),
        name="proj",
    )(x, g, w_packed, gk2)


def _norm_matmul_kernel(x_ref, g_ref, w_ref, o_ref):
    o_ref[...] = _dot(_rms(x_ref[...], g_ref[...]).astype(BF16), w_ref[...])


def _norm_matmul(x, g, w16):
    n, d = x.shape
    tm = min(n, 512)
    return pl.pallas_call(
        _norm_matmul_kernel,
        grid=(n // tm,),
        in_specs=[pl.BlockSpec((tm, d), lambda i: (i, 0)),
                  pl.BlockSpec((1, d), lambda i: (0, 0)),
                  pl.BlockSpec(w16.shape, lambda i: (0, 0))],
        out_specs=pl.BlockSpec((tm, w16.shape[1]), lambda i: (i, 0)),
        out_shape=jax.ShapeDtypeStruct((n, w16.shape[1]), F32),
        compiler_params=_cparams(("arbitrary",)),
        name="norm_matmul",
    )(x, g, w16)


def _split_halves(q_ref, dst_ref, n_pairs):
    t = q_ref.shape[0]
    lo = lax.broadcasted_iota(I32, (t, LANES), 1) < HEAD_DIM
    for p in range(n_pairs):
        qp = q_ref[:, p * LANES:(p + 1) * LANES]
        dst_ref[2 * p] = jnp.where(lo, qp, jnp.zeros_like(qp))
        dst_ref[2 * p + 1] = jnp.where(lo, jnp.zeros_like(qp), qp)


def _online_softmax_step(s, h, m_ref, l_ref):
    m_prev = m_ref[h]
    m_new = jnp.maximum(m_prev, jnp.max(s, axis=1, keepdims=True))
    alpha = jnp.exp(m_prev - m_new)
    p = jnp.exp(s - jnp.concatenate([m_new] * (s.shape[1] // LANES), axis=1))
    l_ref[h] = alpha * l_ref[h] + jnp.sum(p, axis=1, keepdims=True)
    m_ref[h] = m_new
    return p, alpha


def _online_softmax_step_t(s, h, m_ref, l_ref):
    m_prev = m_ref[h]
    m_new = jnp.maximum(m_prev, jnp.max(s, axis=0, keepdims=True))
    alpha = jnp.exp(m_prev - m_new)
    p = jnp.exp(s - m_new)
    l_ref[h] = alpha * l_ref[h] + jnp.sum(p, axis=0, keepdims=True)
    m_ref[h] = m_new
    return p, alpha


def _t5_bucket(dist):
    n = jnp.maximum(dist, 0)
    nf = jnp.maximum(n, 1).astype(F32)
    large = MAX_EXACT + (jnp.log(nf / MAX_EXACT) / math.log(MAX_DISTANCE / MAX_EXACT)
                         * (N_BUCKETS - MAX_EXACT)).astype(I32)
    large = jnp.minimum(large, N_BUCKETS - 1)
    return jnp.where(n < MAX_EXACT, n, large)


def _bias_lookup(table, dist):
    onehot = jax.nn.one_hot(_t5_bucket(dist), N_BUCKETS, dtype=F32)
    return jnp.dot(onehot, table.astype(F32), precision=lax.Precision.HIGHEST)


def _prompt_bias_tiles(table, t):
    assert t >= MAX_DISTANCE
    r = jnp.arange(t, dtype=I32)[None, :]
    c = jnp.arange(t, dtype=I32)[:, None]
    tiles = []
    for delta in range(3):
        dist = delta * t + r - c
        b = jnp.transpose(_bias_lookup(table, dist), (2, 0, 1))
        tiles.append(jnp.where(dist[None] >= 0, b, NEG))
    return jnp.stack(tiles)


def _dsa_prompt_kernel(k_sel, n_bits, qa_ref, qi_ref, wi_ref, k2_ref, k_ref, vt_ref, bt_ref, o_ref,
                       qah_ref, qih_ref, key_ref, m_ref, l_ref, acc_ref):
    t = qa_ref.shape[0]
    i = pl.program_id(1)
    nj = i + 1
    kidx = lax.broadcasted_iota(I32, (t, t), 0)
    qidx = lax.broadcasted_iota(I32, (t, t), 1)

    _split_halves(qa_ref, qah_ref, N_HEADS_A // 2)
    _split_halves(qi_ref, qih_ref, N_IDX_HEADS // 2)
    wi_t = wi_ref[...].T

    def score_chunk(j, carry):
        k0 = pl.multiple_of(j * t, t)
        kc = k2_ref[pl.ds(k0, t), :]
        acc = jnp.zeros((t, t), F32)
        for h in range(N_IDX_HEADS):
            acc = acc + jnp.maximum(_dot_nt(kc, qih_ref[h]), 0.0) * wi_t[h:h + 1, :]
        key = _sortable_key(acc)
        key_ref[j] = jnp.where(k0 + kidx <= i * t + qidx, key, jnp.int32(INT_MIN))
        return carry

    lax.fori_loop(0, nj, score_chunk, 0)

    def count_where(pred_fn):
        def body(j, c):
            hit = jnp.where(pred_fn(key_ref[j], j), 1.0, 0.0)
            return c + jnp.sum(hit.reshape(t // 8, 8, t), axis=0)
        return jnp.sum(lax.fori_loop(0, nj, body, jnp.zeros((8, t), F32)), axis=0, keepdims=True)

    k_f = float(k_sel)
    n_valid = (i * t + lax.broadcasted_iota(I32, (1, t), 1) + 1).astype(F32)

    def search_cond(state):
        b, _, cnt = state
        return (b < 32) & (jnp.max(cnt) > k_f)

    def search_step(state):
        b, tu, cnt = state
        cand_u = tu | lax.shift_left(jnp.int32(1), 31 - b)
        cand = cand_u ^ jnp.int32(INT_MIN)
        c = count_where(lambda kc, j: kc >= cand)
        take = c >= k_f
        return b + 1, jnp.where(take, cand_u, tu), jnp.where(take, c, cnt)

    _, tu, cnt = lax.while_loop(search_cond, search_step, (jnp.int32(0), jnp.zeros((1, t), I32), n_valid))
    thr = tu ^ jnp.int32(INT_MIN)

    def tie_search(_):
        need = k_f - count_where(lambda kc, j: kc > thr)

        def cut_step(b, c0):
            cand = c0 | lax.shift_left(jnp.int32(1), n_bits - 1 - b)
            c = count_where(lambda kc, j: (kc == thr) & (j * t + kidx < cand))
            return jnp.where(c < need, cand, c0)
        return lax.fori_loop(0, n_bits, cut_step, jnp.zeros((1, t), I32))

    cut = lax.cond(jnp.max(cnt) > k_f, tie_search, lambda _: jnp.full((1, t), 2 ** 30, I32), 0)

    m_ref[...] = jnp.full(m_ref.shape, NEG, F32)
    l_ref[...] = jnp.zeros(l_ref.shape, F32)
    acc_ref[...] = jnp.zeros(acc_ref.shape, F32)

    def attn_chunk(j, carry):
        k0 = pl.multiple_of(j * t, t)
        kc = key_ref[j]
        sel = (kc > thr) | ((kc == thr) & (k0 + kidx <= cut))
        dsel = jnp.minimum(i - j, 2)
        for ks in range(0, t, KEY_SUB):
            keys = slice(ks, ks + KEY_SUB)
            for p in range(N_HEADS_A // 2):
                kp = k_ref[pl.ds(k0 + ks, KEY_SUB), p * LANES:(p + 1) * LANES]
                vp_t = vt_ref[j, p * LANES:(p + 1) * LANES, keys]
                for half in range(2):
                    h = 2 * p + half
                    s = jnp.where(sel[keys], _dot_nt(kp, qah_ref[h]) + bt_ref[dsel, h, keys, :], NEG)
                    pr, alpha = _online_softmax_step_t(s, h, m_ref, l_ref)
                    pv = _dot(vp_t, pr.astype(BF16))
                    rows = slice(half * HEAD_DIM, (half + 1) * HEAD_DIM)
                    acc_ref[p, rows, :] = acc_ref[p, rows, :] * alpha + pv[rows]
        return carry

    lax.fori_loop(0, nj, attn_chunk, 0)

    for p in range(N_HEADS_A // 2):
        o_t = jnp.concatenate([acc_ref[p, :HEAD_DIM, :] / l_ref[2 * p],
                               acc_ref[p, HEAD_DIM:, :] / l_ref[2 * p + 1]], axis=0)
        o_ref[:, p * LANES:(p + 1) * LANES] = o_t.T.astype(BF16)


def _dsa_prompt(qa, qi, wi, ki16, ka16, vat, bias_tiles, batch, seq):
    t = min(ATTN_TILE, seq)
    nq = seq // t
    k_sel = min(TOPK_MAX, seq // 4)
    n_bits = max(1, (seq - 1).bit_length())
    kern = functools.partial(_dsa_prompt_kernel, k_sel, n_bits)
    return pl.pallas_call(
        kern,
        grid=(batch, nq),
        in_specs=[pl.BlockSpec((t, A_WIDTH), lambda b, i: (b * nq + i, 0)),
                  pl.BlockSpec((t, A_WIDTH), lambda b, i: (b * nq + i, 0)),
                  pl.BlockSpec((t, LANES), lambda b, i: (b * nq + i, 0)),
                  pl.BlockSpec((seq, LANES), lambda b, i: (b, 0)),
                  pl.BlockSpec((seq, A_WIDTH), lambda b, i: (b, 0)),
                  pl.BlockSpec((nq, A_WIDTH, t), lambda b, i: (b, 0, 0)),
                  pl.BlockSpec(bias_tiles.shape, lambda b, i: (0, 0, 0, 0))],
        out_specs=pl.BlockSpec((t, A_WIDTH), lambda b, i: (b * nq + i, 0)),
        out_shape=jax.ShapeDtypeStruct((batch * seq, A_WIDTH), BF16),
        scratch_shapes=[pltpu.VMEM((N_HEADS_A, t, LANES), BF16),
                        pltpu.VMEM((N_IDX_HEADS, t, LANES), BF16),
                        pltpu.VMEM((nq, t, t), I32),
                        pltpu.VMEM((N_HEADS_A, 1, t), F32),
                        pltpu.VMEM((N_HEADS_A, 1, t), F32),
                        pltpu.VMEM((N_HEADS_A // 2, LANES, t), F32)],
        compiler_params=_cparams(("arbitrary", "arbitrary")),
        name="dsa_prompt",
    )(qa, qi, wi, ki16, ka16, vat, bias_tiles)


def _subln(o0, o1, lam, lam_init, g):
    d = o0 - lam * o1
    return _rms(d, g) * (1.0 - lam_init)


def _diff_prompt_kernel(lam_init, lam_ref, qb_ref, kb_ref, vbt_ref, bt_ref, g_ref, o_ref,
                        qh_ref, m_ref, l_ref, acc_ref):
    t = qb_ref.shape[0]
    i = pl.program_id(1)
    _split_halves(qb_ref, qh_ref, N_HEADS_B)
    m_ref[...] = jnp.full(m_ref.shape, NEG, F32)
    l_ref[...] = jnp.zeros(l_ref.shape, F32)
    acc_ref[...] = jnp.zeros(acc_ref.shape, F32)

    def chunk(j, carry):
        k0 = pl.multiple_of(j * t, t)
        dsel = jnp.minimum(i - j, 2)
        for ks in range(0, t, KEY_SUB):
            keys = slice(ks, ks + KEY_SUB)
            for h in range(N_HEADS_B):
                kp = kb_ref[pl.ds(k0 + ks, KEY_SUB), h * LANES:(h + 1) * LANES]
                vp_t = vbt_ref[j, h * LANES:(h + 1) * LANES, keys]
                for mp in range(2):
                    c = 2 * h + mp
                    s = _dot_nt(kp, qh_ref[c]) + bt_ref[dsel, c, keys, :]
                    pr, alpha = _online_softmax_step_t(s, c, m_ref, l_ref)
                    acc_ref[c] = acc_ref[c] * alpha + _dot(vp_t, pr.astype(BF16))
        return carry

    lax.fori_loop(0, i + 1, chunk, 0)

    lam = lam_ref[0]
    for h in range(N_HEADS_B):
        o0 = (acc_ref[2 * h] / l_ref[2 * h]).T
        o1 = (acc_ref[2 * h + 1] / l_ref[2 * h + 1]).T
        o_ref[:, h * LANES:(h + 1) * LANES] = _subln(o0, o1, lam, lam_init, g_ref[...]).astype(BF16)


def _diff_prompt(lam, lam_init, qb, kb16, vbt, bias_tiles, g_sub, batch, seq):
    t = min(ATTN_TILE, seq)
    nq = seq // t
    n_maps = 2 * N_HEADS_B
    kern = functools.partial(_diff_prompt_kernel, lam_init)
    return pl.pallas_call(
        kern,
        grid=(batch, nq),
        in_specs=[pl.BlockSpec(memory_space=pltpu.SMEM),
                  pl.BlockSpec((t, B_WIDTH), lambda b, i: (b * nq + i, 0)),
                  pl.BlockSpec((seq, B_WIDTH), lambda b, i: (b, 0)),
                  pl.BlockSpec((nq, B_WIDTH, t), lambda b, i: (b, 0, 0)),
                  pl.BlockSpec(bias_tiles.shape, lambda b, i: (0, 0, 0, 0)),
                  pl.BlockSpec((1, LANES), lambda b, i: (0, 0))],
        out_specs=pl.BlockSpec((t, B_WIDTH), lambda b, i: (b * nq + i, 0)),
        out_shape=jax.ShapeDtypeStruct((batch * seq, B_WIDTH), BF16),
        scratch_shapes=[pltpu.VMEM((n_maps, t, LANES), BF16),
                        pltpu.VMEM((n_maps, 1, t), F32),
                        pltpu.VMEM((n_maps, 1, t), F32),
                        pltpu.VMEM((n_maps, LANES, t), F32)],
        compiler_params=_cparams(("arbitrary", "arbitrary")),
        name="diff_prompt",
    )(lam, qb, kb16, vbt, bias_tiles, g_sub)


SAMPLE_ROWS = 8


def _sample_scores_kernel(n_pages, g_pages, dec_seq, pt_ref, *refs):
    page_refs = refs[:g_pages]
    qx_ref, wx_ref, knew_ref, key_ref = refs[g_pages:]
    del pt_ref
    step = pl.program_id(1)
    n_steps = n_pages // g_pages
    r8 = SAMPLE_ROWS

    def chunk_scores(kt16):
        d = jnp.maximum(_dot(qx_ref[0], kt16), 0.0) * wx_ref[0]
        acc = d[0:r8]
        for h in range(1, N_IDX_HEADS):
            acc = acc + d[h * r8:(h + 1) * r8]
        return _sortable_key(acc)

    for g in range(g_pages):
        key_ref[0, step * g_pages + g] = chunk_scores(page_refs[g][0].astype(BF16))

    @pl.when(step == n_steps - 1)
    def _():
        row = lax.broadcasted_iota(I32, (r8, LANES), 0)
        lane = lax.broadcasted_iota(I32, (r8, LANES), 1)
        key_new = chunk_scores(knew_ref[0])
        key_ref[0, n_pages] = jnp.where((lane <= row) & (lane < dec_seq), key_new, jnp.int32(INT_MIN))


def _sample_scores(page_table, cache_kidx_t, qx, wx, knew_t, dec_seq, g_pages):
    batch, n_pages = page_table.shape
    n_steps = n_pages // g_pages
    rows = N_IDX_HEADS * SAMPLE_ROWS
    kern = functools.partial(_sample_scores_kernel, n_pages, g_pages, dec_seq)
    page_specs = [pl.BlockSpec((1, D_IDX, LANES), functools.partial(
        lambda b, s, pt, g: (pt[b, s * g_pages + g], 0, 0), g=g)) for g in range(g_pages)]
    grid_spec = pltpu.PrefetchScalarGridSpec(
        num_scalar_prefetch=1,
        grid=(batch, n_steps),
        in_specs=page_specs + [pl.BlockSpec((1, rows, D_IDX), lambda b, s, pt: (b, 0, 0)),
                               pl.BlockSpec((1, rows, LANES), lambda b, s, pt: (b, 0, 0)),
                               pl.BlockSpec((1, D_IDX, LANES), lambda b, s, pt: (b, 0, 0))],
        out_specs=pl.BlockSpec((1, n_pages + 1, SAMPLE_ROWS, LANES), lambda b, s, pt: (b, 0, 0, 0)))
    return pl.pallas_call(
        kern,
        grid_spec=grid_spec,
        out_shape=jax.ShapeDtypeStruct((batch, n_pages + 1, SAMPLE_ROWS, LANES), I32),
        compiler_params=_cparams(("arbitrary", "arbitrary")),
        name="sample_scores",
    )(page_table, *([cache_kidx_t] * g_pages), qx, wx, knew_t)


def _sample_select_kernel(k_sel, n_bits, key_ref, mask_ref):
    shape = key_ref.shape
    colg = lax.broadcasted_iota(I32, shape, 1) * LANES + lax.broadcasted_iota(I32, shape, 3)

    def count(pred):
        c = jnp.sum(jnp.where(pred, 1.0, 0.0), axis=1, keepdims=True)
        return jnp.sum(c, axis=3, keepdims=True)

    stat = (shape[0], 1, shape[2], 1)

    def bit_step(b, tu):
        cand_u = tu | lax.shift_left(jnp.int32(1), 31 - b)
        cand = cand_u ^ jnp.int32(INT_MIN)
        return jnp.where(count(key_ref[...] >= cand) >= k_sel, cand_u, tu)

    tu = lax.fori_loop(0, 32, bit_step, jnp.zeros(stat, I32))
    thr = tu ^ jnp.int32(INT_MIN)
    need = k_sel - count(key_ref[...] > thr)

    def cut_step(b, c0):
        cand = c0 | lax.shift_left(jnp.int32(1), n_bits - 1 - b)
        return jnp.where(count((key_ref[...] == thr) & (colg < cand)) < need, cand, c0)

    cut = lax.fori_loop(0, n_bits, cut_step, jnp.zeros(stat, I32))
    keys = key_ref[...]
    sel = (keys > thr) | ((keys == thr) & (colg <= cut))
    sel = sel & (keys != jnp.int32(INT_MIN))
    mask_ref[...] = jnp.where(sel, 0.0, NEG)


def _sample_select(keys, k_sel):
    batch, n_chunks = keys.shape[:2]
    bt = 8 if batch % 8 == 0 else 1
    n_bits = max(1, (n_chunks * LANES - 1).bit_length())
    blk = (bt, n_chunks, SAMPLE_ROWS, LANES)
    return pl.pallas_call(
        functools.partial(_sample_select_kernel, k_sel, n_bits),
        grid=(batch // bt,),
        in_specs=[pl.BlockSpec(blk, lambda i: (i, 0, 0, 0))],
        out_specs=pl.BlockSpec(blk, lambda i: (i, 0, 0, 0)),
        out_shape=jax.ShapeDtypeStruct(keys.shape, F32),
        compiler_params=_cparams(("arbitrary",)),
        name="sample_select",
    )(keys)


def _sample_attn_kernel(lam_init, n_pages, g_pages, pt_ref, lam_ref, *refs):
    kva_refs = refs[:g_pages]
    kb_refs = refs[g_pages:2 * g_pages]
    vb_refs = refs[2 * g_pages:3 * g_pages]
    (qa_ref, qb_ref, mask_ref, maskn_ref, ba_ref, bb_ref, kvan_ref, kbn_ref, vbn_ref, g_ref,
     o_ref, m_ref, l_ref, acca_ref, accb_ref) = refs[3 * g_pages:]
    del pt_ref
    step = pl.program_id(1)
    n_steps = n_pages // g_pages
    r8 = SAMPLE_ROWS

    @pl.when(step == 0)
    def _():
        m_ref[...] = jnp.full(m_ref.shape, NEG, F32)
        l_ref[...] = jnp.zeros(l_ref.shape, F32)
        acca_ref[...] = jnp.zeros(acca_ref.shape, F32)
        accb_ref[...] = jnp.zeros(accb_ref.shape, F32)

    def both(kva, kb, vb, mask8, bsel):
        n = len(kva)
        cols = [slice(g * LANES, (g + 1) * LANES) for g in range(n)]
        s = jnp.concatenate(
            [_dot(qa_ref[0], kva[g][0, :A_WIDTH, :].astype(BF16))
             + (ba_ref[bsel[g]] + jnp.concatenate([mask8[g]] * N_HEADS_A, axis=0)) for g in range(n)], axis=1)
        pr, alpha = _online_softmax_step(s, 0, m_ref, l_ref)
        pr = pr.astype(BF16)
        pv = _dot_nt(pr[:, cols[0]], kva[0][0, A_WIDTH:, :].astype(BF16))
        for g in range(1, n):
            pv = pv + _dot_nt(pr[:, cols[g]], kva[g][0, A_WIDTH:, :].astype(BF16))
        acca_ref[...] = acca_ref[...] * jnp.concatenate([alpha] * (A_WIDTH // LANES), axis=1) + pv
        s = jnp.concatenate([_dot(qb_ref[0], kb[g][0].astype(BF16)) + bb_ref[bsel[g]] for g in range(n)], axis=1)
        pr, alpha = _online_softmax_step(s, 1, m_ref, l_ref)
        pr = pr.astype(BF16)
        pv = []
        for h in range(N_HEADS_B):
            rows = slice(2 * r8 * h, 2 * r8 * (h + 1))
            acc = None
            for g in range(n):
                d = _dot(pr[rows, cols[g]], vb[g][0, pl.ds(h, LANES, stride=N_HEADS_B), :].astype(BF16))
                acc = d if acc is None else acc + d
            pv.append(acc)
        accb_ref[...] = accb_ref[...] * alpha + jnp.concatenate(pv, axis=0)

    bsel = [0] * (g_pages - 1) + [jnp.where(step == n_steps - 1, 1, 0)]
    both(kva_refs, kb_refs, vb_refs, [mask_ref[0, g] for g in range(g_pages)], bsel)

    @pl.when(step == n_steps - 1)
    def _():
        both([kvan_ref], [kbn_ref], [vbn_ref], [maskn_ref[0, 0]], [2])
        lane = lax.broadcasted_iota(I32, (r8, A_WIDTH), 1)
        oa = jnp.zeros((r8, A_WIDTH), F32)
        la = jnp.concatenate([l_ref[0]] * (A_WIDTH // LANES), axis=1)
        acca = acca_ref[...] / la
        for h in range(N_HEADS_A):
            blk = acca[h * r8:(h + 1) * r8]
            oa = jnp.where((lane >= h * HEAD_DIM) & (lane < (h + 1) * HEAD_DIM), blk, oa)
        o_ref[0, :, :A_WIDTH] = oa.astype(BF16)
        lam = lam_ref[0]
        ob = accb_ref[...] / l_ref[1]
        for h in range(N_HEADS_B):
            r0 = 2 * h * r8
            o_ref[0, :, A_WIDTH + h * LANES:A_WIDTH + (h + 1) * LANES] = _subln(
                ob[r0:r0 + r8], ob[r0 + r8:r0 + 2 * r8], lam, lam_init, g_ref[...]).astype(BF16)


def _sample_attn(lam, lam_init, page_table, cache_kva_t, cache_kb_t, cache_vb, qa_x, qb_x, mask, bias_a, bias_b,
                 kva_new_t, kb_new_t, vb_new, g_sub, g_pages):
    batch, n_pages = page_table.shape
    n_steps = n_pages // g_pages
    rows = N_HEADS_A * SAMPLE_ROWS
    kern = functools.partial(_sample_attn_kernel, lam_init, n_pages, g_pages)

    def page_spec(width, g):
        return pl.BlockSpec((1, width, LANES), lambda b, s, pt: (pt[b, s * g_pages + g], 0, 0))

    in_specs = ([pl.BlockSpec(memory_space=pltpu.SMEM)]
                + [page_spec(2 * A_WIDTH, g) for g in range(g_pages)]
                + [page_spec(B_WIDTH, g) for g in range(g_pages)]
                + [page_spec(B_WIDTH, g) for g in range(g_pages)]
                + [pl.BlockSpec((1, rows, A_WIDTH), lambda b, s, pt: (b, 0, 0)),
                   pl.BlockSpec((1, rows, B_WIDTH), lambda b, s, pt: (b, 0, 0)),
                   pl.BlockSpec((1, g_pages, SAMPLE_ROWS, LANES), lambda b, s, pt: (b, s, 0, 0)),
                   pl.BlockSpec((1, 1, SAMPLE_ROWS, LANES), lambda b, s, pt: (b, n_pages, 0, 0)),
                   pl.BlockSpec(bias_a.shape, lambda b, s, pt: (0, 0, 0)),
                   pl.BlockSpec(bias_b.shape, lambda b, s, pt: (0, 0, 0)),
                   pl.BlockSpec((1, 2 * A_WIDTH, LANES), lambda b, s, pt: (b, 0, 0)),
                   pl.BlockSpec((1, B_WIDTH, LANES), lambda b, s, pt: (b, 0, 0)),
                   pl.BlockSpec((1, B_WIDTH, LANES), lambda b, s, pt: (b, 0, 0)),
                   pl.BlockSpec((1, LANES), lambda b, s, pt: (0, 0))])
    grid_spec = pltpu.PrefetchScalarGridSpec(
        num_scalar_prefetch=1,
        grid=(batch, n_steps),
        in_specs=in_specs,
        out_specs=pl.BlockSpec((1, SAMPLE_ROWS, A_WIDTH + B_WIDTH), lambda b, s, pt: (b, 0, 0)),
        scratch_shapes=[pltpu.VMEM((2, rows, LANES), F32),
                        pltpu.VMEM((2, rows, LANES), F32),
                        pltpu.VMEM((rows, A_WIDTH), F32),
                        pltpu.VMEM((rows, LANES), F32)])
    return pl.pallas_call(
        kern,
        grid_spec=grid_spec,
        out_shape=jax.ShapeDtypeStruct((batch, SAMPLE_ROWS, A_WIDTH + B_WIDTH), BF16),
        compiler_params=_cparams(("arbitrary", "arbitrary")),
        name="sample_attn",
    )(page_table, lam, *([cache_kva_t] * g_pages), *([cache_kb_t] * g_pages), *([cache_vb] * g_pages),
      qa_x, qb_x, mask, mask, bias_a, bias_b, kva_new_t, kb_new_t, vb_new, g_sub)


def _sample_bias(table, past, dec_seq, causal_new):
    t = jnp.arange(SAMPLE_ROWS, dtype=I32)[:, None]
    c = jnp.arange(LANES, dtype=I32)[None, :]
    far = jnp.full((SAMPLE_ROWS, LANES), MAX_DISTANCE, I32)
    last = t + LANES - c
    new = t - c
    out = []
    for dist in (far, last, new):
        b = jnp.transpose(_bias_lookup(table, dist), (2, 0, 1))
        out.append(b.reshape(table.shape[1] * SAMPLE_ROWS, LANES))
    if causal_new:
        ok = jnp.tile((new >= 0) & (c < dec_seq), (table.shape[1], 1))
        out[2] = jnp.where(ok, out[2], NEG)
    del past
    return jnp.stack(out)


def _block_diag_rows(q):
    b, t = q.shape[:2]
    q = jnp.pad(q, ((0, 0), (0, SAMPLE_ROWS - t), (0, 0), (0, 0)))
    q = jnp.transpose(q, (0, 2, 1, 3))
    eye = jnp.eye(8, dtype=q.dtype)
    return jnp.einsum('bctd,cg->bctgd', q, eye).reshape(b, 8 * SAMPLE_ROWS, 8 * HEAD_DIM)


def _tail1_kernel(n_mix, x_ref, *refs):
    o_refs = refs[:n_mix]
    w_refs = refs[n_mix:2 * n_mix]
    g_ref, wq_ref, x1_ref, qm_ref = refs[2 * n_mix:]
    x1 = x_ref[...]
    for o_ref, w_ref in zip(o_refs, w_refs):
        x1 = x1 + _dot(o_ref[...], w_ref[...])
    x1_ref[...] = x1
    hm = _rms(x1, g_ref[...]).astype(BF16)
    qm_ref[...] = (_dot(hm, wq_ref[...]) * HEAD_DIM ** -0.5).astype(BF16)


def _tail1(x, mixes, g_mem, wq16):
    n, d = x.shape
    tm = min(n, 512)
    n_mix = len(mixes)
    mw = wq16.shape[1]
    return pl.pallas_call(
        functools.partial(_tail1_kernel, n_mix),
        grid=(n // tm,),
        in_specs=([pl.BlockSpec((tm, d), lambda i: (i, 0))]
                  + [pl.BlockSpec((tm, o.shape[1]), lambda i: (i, 0)) for o, _ in mixes]
                  + [pl.BlockSpec(w.shape, lambda i: (0, 0)) for _, w in mixes]
                  + [pl.BlockSpec((1, d), lambda i: (0, 0)),
                     pl.BlockSpec(wq16.shape, lambda i: (0, 0))]),
        out_specs=[pl.BlockSpec((tm, d), lambda i: (i, 0)), pl.BlockSpec((tm, mw), lambda i: (i, 0))],
        out_shape=[jax.ShapeDtypeStruct((n, d), F32), jax.ShapeDtypeStruct((n, mw), BF16)],
        compiler_params=_cparams(("arbitrary",)),
        name="tail1",
    )(x, *[o for o, _ in mixes], *[w for _, w in mixes], g_mem, wq16)


def _tail2_kernel(qm_ref, mkv_ref, x1_ref, wo_ref, g_ref, wr_ref, br_ref, x2_ref, h_ref, route_ref):
    tq = qm_ref.shape[1]
    mem_w = N_HEADS_MEM * HEAD_DIM
    lo = lax.broadcasted_iota(I32, (tq, LANES), 1) < HEAD_DIM
    mkv = mkv_ref[0].astype(BF16)
    outs = []
    for p in range(N_HEADS_MEM // 2):
        qp = qm_ref[0, :, p * LANES:(p + 1) * LANES]
        kp = mkv[:, p * LANES:(p + 1) * LANES]
        vp = mkv[:, mem_w + p * LANES:mem_w + (p + 1) * LANES]
        pv = []
        for half in range(2):
            qh = jnp.where(lo == (half == 0), qp, jnp.zeros_like(qp))
            s = _dot_nt(qh, kp)
            e = jnp.exp(s - jnp.max(s, axis=1, keepdims=True))
            pv.append(_dot(e.astype(BF16), vp) / jnp.sum(e, axis=1, keepdims=True))
        outs.append(jnp.where(lo, pv[0], pv[1]))
    om = jnp.concatenate(outs, axis=1).astype(BF16)
    x2 = x1_ref[0] + _dot(om, wo_ref[...])
    x2_ref[0] = x2
    h = _rms(x2, g_ref[...])
    h_ref[0] = h.astype(BF16)

    lg = jnp.dot(h, wr_ref[...], preferred_element_type=F32, precision=lax.Precision.HIGHEST) + br_ref[...]
    lane = lax.broadcasted_iota(I32, (tq, LANES), 1)
    lane_f = lane.astype(F32)
    big = jnp.float32(1e9)
    is_g = lane < N_GROUPS
    m1 = jnp.max(jnp.where(is_g, lg, -jnp.inf), axis=1, keepdims=True)
    grp = jnp.min(jnp.where(is_g & (lg == m1), lane_f, big), axis=1, keepdims=True)
    p_grp = 1.0 / jnp.sum(jnp.where(is_g, jnp.exp(lg - m1), 0.0), axis=1, keepdims=True)
    e0 = N_GROUPS + grp * EXPERTS_PER_GROUP
    in_g = (lane_f >= e0) & (lane_f < e0 + EXPERTS_PER_GROUP)
    v1 = jnp.max(jnp.where(in_g, lg, -jnp.inf), axis=1, keepdims=True)
    i1 = jnp.min(jnp.where(in_g & (lg == v1), lane_f, big), axis=1, keepdims=True)
    rest = in_g & (lane_f != i1)
    v2 = jnp.max(jnp.where(rest, lg, -jnp.inf), axis=1, keepdims=True)
    i2 = jnp.min(jnp.where(rest & (lg == v2), lane_f, big), axis=1, keepdims=True)
    e2 = jnp.exp(v2 - v1)
    den = 1.0 + e2
    g1 = (1.0 / den) * p_grp
    g2 = (e2 / den) * p_grp
    route = jnp.where(lane == 0, i1 - N_GROUPS, 0.0)
    route = jnp.where(lane == 1, i2 - N_GROUPS, route)
    route = jnp.where(lane == 2, g1, route)
    route = jnp.where(lane == 3, g2, route)
    route_ref[0] = route


def _tail2(qm, mkv, x1, wo16, g_ffn, wr, br):
    b, t, d = x1.shape
    tq = min(t, 512)
    mem_w = qm.shape[2]
    n_mem = mkv.shape[1]
    return pl.pallas_call(
        _tail2_kernel,
        grid=(b, t // tq),
        in_specs=[pl.BlockSpec((1, tq, mem_w), lambda i, j: (i, j, 0)),
                  pl.BlockSpec((1, n_mem, 2 * mem_w), lambda i, j: (i, 0, 0)),
                  pl.BlockSpec((1, tq, d), lambda i, j: (i, j, 0)),
                  pl.BlockSpec(wo16.shape, lambda i, j: (0, 0)),
                  pl.BlockSpec((1, d), lambda i, j: (0, 0)),
                  pl.BlockSpec(wr.shape, lambda i, j: (0, 0)),
                  pl.BlockSpec((1, LANES), lambda i, j: (0, 0))],
        out_specs=[pl.BlockSpec((1, tq, d), lambda i, j: (i, j, 0)),
                   pl.BlockSpec((1, tq, d), lambda i, j: (i, j, 0)),
                   pl.BlockSpec((1, tq, LANES), lambda i, j: (i, j, 0))],
        out_shape=[jax.ShapeDtypeStruct((b, t, d), F32), jax.ShapeDtypeStruct((b, t, d), BF16),
                   jax.ShapeDtypeStruct((b, t, LANES), F32)],
        compiler_params=_cparams(("arbitrary", "arbitrary")),
        name="tail2",
    )(qm, mkv, x1, wo16, g_ffn, wr, br)


FLAG_FIRST, FLAG_LAST, FLAG_VALID = 1, 2, 4


def _moe_ffn_kernel(pb_ref, pc_ref, fl_ref, be_ref, h_ref, rt_ref, gate_ref, wgu_ref, wdn_ref, y_ref, acc_ref):
    del pb_ref, be_ref
    p = pl.program_id(0)
    fl = fl_ref[p]
    c = h_ref.shape[0]
    r = acc_ref.shape[0]

    @pl.when((fl & FLAG_FIRST) != 0)
    def _():
        acc_ref[...] = jnp.zeros(acc_ref.shape, F32)

    @pl.when((fl & FLAG_VALID) != 0)
    def _():
        rel = rt_ref[...] - pc_ref[p] * c
        onehot = jnp.where(rel == lax.broadcasted_iota(I32, (r, c), 1), 1.0, 0.0).astype(BF16)
        acc_ref[...] += _dot(onehot, h_ref[...])

    @pl.when((fl & FLAG_LAST) != 0)
    def _():
        gu = _dot(acc_ref[...].astype(BF16), wgu_ref[0])
        gate = gu[:, :D_EXPERT]
        up = gu[:, D_EXPERT:]
        act = gate * (1.0 / (1.0 + jnp.exp(-gate))) * up
        y_ref[...] = (_dot(act.astype(BF16), wdn_ref[0]) * gate_ref[...]).astype(BF16)


def _moe_ffn(pb, pc, flags, block_expert, h16, row_tok, row_gate, wgu16, wdn16, chunk):
    n_rows = row_tok.shape[0]
    d = h16.shape[1]
    r = EXPERT_ROWS
    grid_spec = pltpu.PrefetchScalarGridSpec(
        num_scalar_prefetch=4,
        grid=(pb.shape[0],),
        in_specs=[pl.BlockSpec((chunk, d), lambda p, pb, pc, fl, be: (pc[p], 0)),
                  pl.BlockSpec((r, 1), lambda p, pb, pc, fl, be: (pb[p], 0)),
                  pl.BlockSpec((r, 1), lambda p, pb, pc, fl, be: (pb[p], 0)),
                  pl.BlockSpec((1,) + wgu16.shape[1:], lambda p, pb, pc, fl, be: (be[pb[p]], 0, 0)),
                  pl.BlockSpec((1,) + wdn16.shape[1:], lambda p, pb, pc, fl, be: (be[pb[p]], 0, 0))],
        out_specs=pl.BlockSpec((r, d), lambda p, pb, pc, fl, be: (pb[p], 0)),
        scratch_shapes=[pltpu.VMEM((r, d), F32)])
    return pl.pallas_call(
        _moe_ffn_kernel,
        grid_spec=grid_spec,
        out_shape=jax.ShapeDtypeStruct((n_rows, d), BF16),
        compiler_params=_cparams(("arbitrary",)),
        name="moe_ffn",
    )(pb, pc, flags, block_expert, h16, row_tok.reshape(n_rows, 1), row_gate.reshape(n_rows, 1), wgu16, wdn16)


def _moe_combine_kernel(pb_ref, pc_ref, fl_ref, y_ref, rt_ref, x2_ref, g_ref, o_ref, acc_ref):
    del pb_ref
    p = pl.program_id(0)
    fl = fl_ref[p]
    c = acc_ref.shape[0]
    r = y_ref.shape[0]

    @pl.when((fl & FLAG_FIRST) != 0)
    def _():
        acc_ref[...] = jnp.zeros(acc_ref.shape, F32)

    @pl.when((fl & FLAG_VALID) != 0)
    def _():
        rel = rt_ref[0] - pc_ref[p] * c
        onehot_t = jnp.where(rel == lax.broadcasted_iota(I32, (c, r), 0), 1.0, 0.0).astype(BF16)
        acc_ref[...] += _dot(onehot_t, y_ref[...])

    @pl.when((fl & FLAG_LAST) != 0)
    def _():
        o_ref[...] = _rms(x2_ref[...] + acc_ref[...], g_ref[...])


def _moe_combine(pb, pc, flags, y_rows, row_tok, x2, g_final, chunk):
    n, d = x2.shape
    r = EXPERT_ROWS
    n_blocks = y_rows.shape[0] // r
    grid_spec = pltpu.PrefetchScalarGridSpec(
        num_scalar_prefetch=3,
        grid=(pb.shape[0],),
        in_specs=[pl.BlockSpec((r, d), lambda p, pb, pc, fl: (pb[p], 0)),
                  pl.BlockSpec((1, 1, r), lambda p, pb, pc, fl: (pb[p], 0, 0)),
                  pl.BlockSpec((chunk, d), lambda p, pb, pc, fl: (pc[p], 0)),
                  pl.BlockSpec((1, d), lambda p, pb, pc, fl: (0, 0))],
        out_specs=pl.BlockSpec((chunk, d), lambda p, pb, pc, fl: (pc[p], 0)),
        scratch_shapes=[pltpu.VMEM((chunk, d), F32)])
    return pl.pallas_call(
        _moe_combine_kernel,
        grid_spec=grid_spec,
        out_shape=jax.ShapeDtypeStruct((n, d), F32),
        compiler_params=_cparams(("arbitrary",)),
        name="moe_combine",
    )(pb, pc, flags, y_rows, row_tok.reshape(n_blocks, 1, r), x2, g_final)


def _moe_and_final(x2, h16, route, wgu16, wdn16, g_final):
    n, d = x2.shape
    n_assign = 2 * n
    experts = route[:, :2].astype(I32).reshape(n_assign)
    onehot = (experts[:, None] == jnp.arange(N_EXPERTS, dtype=I32)[None, :]).astype(I32)
    running = jnp.cumsum(onehot, axis=0)
    rank = jnp.sum((running - onehot) * onehot, axis=1)
    counts = running[-1]
    padded = (counts + EXPERT_ROWS - 1) // EXPERT_ROWS * EXPERT_ROWS
    pad_ends = jnp.cumsum(padded)
    pad_starts = pad_ends - padded
    pos = jnp.sum(onehot * pad_starts[None, :], axis=1) + rank
    n_blocks = -(-n_assign // EXPERT_ROWS) + N_EXPERTS
    n_rows = n_blocks * EXPERT_ROWS
    tok = jnp.arange(n_assign, dtype=I32) // 2
    gate_bits = lax.bitcast_convert_type(route[:, 2:4].reshape(n_assign), I32)
    rows = jnp.concatenate([jnp.full((n_rows, 1), -1, I32), jnp.zeros((n_rows, 1), I32)], axis=1)
    rows = rows.at[pos].set(jnp.stack([tok, gate_bits], axis=1))
    row_tok = rows[:, 0]
    row_gate = lax.bitcast_convert_type(rows[:, 1], F32)
    block_start = jnp.arange(n_blocks, dtype=I32) * EXPERT_ROWS
    block_expert = jnp.minimum(jnp.sum((pad_ends[None, :] <= block_start[:, None]).astype(I32), axis=1),
                               N_EXPERTS - 1).astype(I32)

    chunk = min(MOE_CHUNK, n)
    assert n % chunk == 0
    n_chunks = n // chunk
    rt2 = row_tok.reshape(n_blocks, EXPERT_ROWS)
    hi = jnp.max(rt2, axis=1)
    lo = jnp.min(jnp.where(rt2 >= 0, rt2, n), axis=1)
    lo_c = jnp.where(hi < 0, 0, lo // chunk)
    hi_c = jnp.where(hi < 0, 0, hi // chunk)
    n_ch = hi_c - lo_c + 1
    ends = jnp.cumsum(n_ch)
    starts = ends - n_ch
    total = ends[-1]
    n_pairs = N_EXPERTS * n_chunks + n_blocks
    p = jnp.arange(n_pairs, dtype=I32)
    valid = p < total
    pb = jnp.minimum(jnp.sum((ends[None, :] <= p[:, None]).astype(I32), axis=1), n_blocks - 1).astype(I32)
    pc = jnp.where(valid, lo_c[pb] + p - starts[pb], hi_c[n_blocks - 1]).astype(I32)
    flags = (jnp.where(valid & (p == starts[pb]), FLAG_FIRST, 0) + jnp.where(valid & (p == ends[pb] - 1), FLAG_LAST, 0)
             + jnp.where(valid, FLAG_VALID, 0)).astype(I32)
    y_rows = _moe_ffn(pb, pc, flags, block_expert, h16, row_tok, row_gate, wgu16, wdn16, chunk)

    order = jnp.argsort(jnp.where(valid, pc * n_blocks + pb, jnp.int32(2 ** 30)))
    pb2, pc2 = pb[order], pc[order]
    last_valid = jnp.maximum(total - 1, 0)
    pb2 = jnp.where(valid, pb2, pb2[last_valid])
    pc2 = jnp.where(valid, pc2, pc2[last_valid])
    prev_c = jnp.concatenate([jnp.full((1,), -1, I32), pc2[:-1]])
    next_c = jnp.concatenate([pc2[1:], jnp.full((1,), -1, I32)])
    flags2 = (jnp.where(valid & (pc2 != prev_c), FLAG_FIRST, 0)
              + jnp.where(valid & ((pc2 != next_c) | (p == total - 1)), FLAG_LAST, 0)
              + jnp.where(valid, FLAG_VALID, 0)).astype(I32)
    return _moe_combine(pb2, pc2, flags2, y_rows, row_tok, x2, g_final, chunk)


def kernel(x_prompt, x_sample, cache_kv_a, cache_k_idx, cache_k_b, cache_v_b, cache_mem_kv, page_table, mem_prompt, rel_bias, g_mix, w_in, g_kidx, lambda_q1, lambda_k1, lambda_q2, lambda_k2, g_subln, w_out, g_mem, g_memin, w_mem_q, w_mem_kv, w_mem_o, g_ffn, w_router1, b_router1, w_router2, b_router2, w_gate_up, w_down, g_final):
    depth = g_mix.shape[0]
    assert depth == 1
    layer = 0
    batch, seq, d = x_prompt.shape
    dec_batch, dec_seq, _ = x_sample.shape
    n_pages = page_table.shape[1]
    n_phys = cache_kv_a.shape[1]
    past = n_pages * LANES
    n_mem = mem_prompt.shape[1]
    mem_w = N_HEADS_MEM * HEAD_DIM
    assert cache_kv_a.shape[2] == LANES and dec_seq <= SAMPLE_ROWS

    table_a = rel_bias[:, :N_HEADS_A]
    table_b = rel_bias[:, N_HEADS_A:]
    lam_init = 0.8 - 0.6 * math.exp(-0.3 * layer)
    lam = (jnp.exp(jnp.sum(lambda_q1[layer] * lambda_k1[layer]).astype(F32))
           - jnp.exp(jnp.sum(lambda_q2[layer] * lambda_k2[layer]).astype(F32)) + lam_init).reshape(1)

    w = w_in[layer]
    offs = [0, 512, 1024, 1536, 2048, 2112, 2120, 2632, 3144, 3656]
    q_a, k_a, v_a, q_i, k_i, w_i, q_b, k_b, v_b = [w[:, offs[s]:offs[s + 1]] for s in range(9)]
    w_packed = jnp.concatenate(
        [q_a, k_a, v_a, q_i, k_i, k_i, jnp.pad(w_i, ((0, 0), (0, LANES - N_IDX_HEADS))), q_b, k_b, v_b],
        axis=1).astype(BF16)
    gk2 = jnp.concatenate([g_kidx[layer], g_kidx[layer]]).reshape(1, LANES)
    g_mix_l = g_mix[layer].reshape(1, d)
    w_out16 = w_out[layer].astype(BF16)
    wq16 = w_mem_q[layer].astype(BF16)
    wo16 = w_mem_o[layer].astype(BF16)
    wr = jnp.concatenate([w_router1[layer],
                          jnp.transpose(w_router2[layer], (1, 0, 2)).reshape(d, N_EXPERTS),
                          jnp.zeros((d, LANES - N_GROUPS - N_EXPERTS), F32)], axis=1)
    br = jnp.concatenate([b_router1[layer], b_router2[layer].reshape(N_EXPERTS),
                          jnp.zeros((LANES - N_GROUPS - N_EXPERTS,), F32)]).reshape(1, LANES)
    wgu16 = w_gate_up[layer].astype(BF16)
    wdn16 = w_down[layer].astype(BF16)
    g_sub = g_subln[layer].reshape(1, LANES)
    g_mem_l = g_mem[layer].reshape(1, d)
    g_ffn_l = g_ffn[layer].reshape(1, d)
    g_fin = g_final.reshape(1, d)

    n_p = batch * seq
    (qa, kva, ka16, vat, qi, ki, ki16, wi, qb, kb, vb, kb16, vbt) = _proj(
        x_prompt.reshape(n_p, d), g_mix_l, w_packed, gk2)
    t_attn = min(ATTN_TILE, seq)
    o_a = _dsa_prompt(qa, qi, wi, ki16, ka16, vat, _prompt_bias_tiles(table_a, t_attn), batch, seq)
    o_b = _diff_prompt(lam, lam_init, qb, kb16, vbt, _prompt_bias_tiles(table_b, t_attn), g_sub, batch, seq)
    mkv_p = _norm_matmul(mem_prompt.reshape(batch * n_mem, d), g_memin[layer].reshape(1, d),
                         w_mem_kv[layer].astype(BF16))
    x1, qm = _tail1(x_prompt.reshape(n_p, d), [(o_a, w_out16[:A_WIDTH]), (o_b, w_out16[A_WIDTH:])],
                    g_mem_l, wq16)
    x2, h_ffn, route = _tail2(qm.reshape(batch, seq, mem_w), mkv_p.reshape(batch, n_mem, 2 * mem_w),
                              x1.reshape(batch, seq, d), wo16, g_ffn_l, wr, br)
    y_prompt = _moe_and_final(x2.reshape(n_p, d), h_ffn.reshape(n_p, d), route.reshape(n_p, LANES),
                              wgu16, wdn16, g_fin).reshape(batch, seq, d)

    n_s = dec_batch * dec_seq
    (qa_s, kva_s, _, _, qi_s, ki_s, ki16_s, wi_s, qb_s, kb_s, vb_s, kb16_s, _) = _proj(
        x_sample.reshape(n_s, d), g_mix_l, w_packed, gk2)
    kva16_s = kva_s.astype(BF16)
    pad_t = SAMPLE_ROWS - dec_seq
    qx = jnp.transpose(jnp.pad(qi_s.reshape(dec_batch, dec_seq, N_IDX_HEADS, D_IDX),
                               ((0, 0), (0, pad_t), (0, 0), (0, 0))), (0, 2, 1, 3)
                       ).reshape(dec_batch, N_IDX_HEADS * SAMPLE_ROWS, D_IDX)
    wx = jnp.transpose(jnp.pad(wi_s[:, :N_IDX_HEADS].reshape(dec_batch, dec_seq, N_IDX_HEADS),
                               ((0, 0), (0, pad_t), (0, 0))), (0, 2, 1)
                       ).reshape(dec_batch, N_IDX_HEADS * SAMPLE_ROWS, 1)
    wx = jnp.broadcast_to(wx, (dec_batch, N_IDX_HEADS * SAMPLE_ROWS, LANES))

    def new_chunk_t(a):
        a = jnp.pad(a.reshape(dec_batch, dec_seq, a.shape[1]), ((0, 0), (0, LANES - dec_seq), (0, 0)))
        return jnp.transpose(a, (0, 2, 1))

    kidx_t = jnp.transpose(cache_k_idx[layer], (0, 2, 1))
    kva_t = jnp.transpose(cache_kv_a[layer], (0, 2, 3, 4, 1)).reshape(n_phys, 2 * A_WIDTH, LANES)
    kb_t = jnp.transpose(cache_k_b[layer], (0, 2, 3, 4, 1)).reshape(n_phys, B_WIDTH, LANES)
    vb_r = cache_v_b[layer].reshape(n_phys, LANES * N_HEADS_B, 2 * HEAD_DIM)
    vb_new = jnp.pad(vb_s.reshape(dec_batch, dec_seq, N_HEADS_B, 2 * HEAD_DIM),
                     ((0, 0), (0, LANES - dec_seq), (0, 0), (0, 0))).reshape(dec_batch, LANES * N_HEADS_B, 2 * HEAD_DIM)

    g_idx = 16 if n_pages % 16 == 0 else 1
    keys = _sample_scores(page_table, kidx_t, qx, wx, new_chunk_t(ki16_s[:, :D_IDX]), dec_seq, g_idx)
    mask = _sample_select(keys, min(TOPK_MAX, (past + dec_seq) // 4))
    g_att = 8 if n_pages % 8 == 0 else 1
    o_s = _sample_attn(
        lam, lam_init, page_table, kva_t, kb_t, vb_r,
        _block_diag_rows(qa_s.reshape(dec_batch, dec_seq, N_HEADS_A, HEAD_DIM)),
        _block_diag_rows(qb_s.reshape(dec_batch, dec_seq, 2 * N_HEADS_B, HEAD_DIM)),
        mask, _sample_bias(table_a, past, dec_seq, False), _sample_bias(table_b, past, dec_seq, True),
        new_chunk_t(kva16_s), new_chunk_t(kb16_s), vb_new, g_sub, g_att)
    o_s = o_s[:, :dec_seq].reshape(n_s, A_WIDTH + B_WIDTH)
    x1_s, qm_s = _tail1(x_sample.reshape(n_s, d), [(o_s, w_out16)], g_mem_l, wq16)
    x2_s, h_s, route_s = _tail2(qm_s.reshape(dec_batch, dec_seq, mem_w),
                                cache_mem_kv[layer].reshape(dec_batch, n_mem, 2 * mem_w),
                                x1_s.reshape(dec_batch, dec_seq, d), wo16, g_ffn_l, wr, br)
    y_sample = _moe_and_final(x2_s.reshape(n_s, d), h_s.reshape(n_s, d), route_s.reshape(n_s, LANES),
                              wgu16, wdn16, g_fin).reshape(dec_batch, dec_seq, d)

    return (y_prompt, y_sample,
            kva.reshape(1, batch, seq, 2, N_HEADS_A, HEAD_DIM),
            ki[:, :D_IDX].reshape(1, batch, seq, D_IDX),
            kb.reshape(1, batch, seq, N_HEADS_B, 2, HEAD_DIM),
            vb.reshape(1, batch, seq, N_HEADS_B, 2 * HEAD_DIM),
            mkv_p.reshape(1, batch, n_mem, 2, N_HEADS_MEM, HEAD_DIM),
            kva_s.reshape(1, dec_batch, dec_seq, 2, N_HEADS_A, HEAD_DIM),
            ki_s[:, :D_IDX].reshape(1, dec_batch, dec_seq, D_IDX),
            kb_s.reshape(1, dec_batch, dec_seq, N_HEADS_B, 2, HEAD_DIM),
            vb_s.reshape(1, dec_batch, dec_seq, N_HEADS_B, 2 * HEAD_DIM))
```

```python
import functools
import math

import jax
import jax.numpy as jnp
from jax import lax
from jax.experimental import pallas as pl
from jax.experimental.pallas import tpu as pltpu

F32 = jnp.float32
BF16 = jnp.bfloat16
I32 = jnp.int32

HEAD_DIM = 64
N_HEADS_A = 8
N_HEADS_B = 4
A_WIDTH = 512
B_WIDTH = 512
N_IDX_HEADS = 8
D_IDX = 64
TOPK_MAX = 256
N_HEADS_MEM = 4
N_BUCKETS = 32
MAX_EXACT = 16
MAX_DISTANCE = 128
N_GROUPS = 4
EXPERTS_PER_GROUP = 8
N_EXPERTS = 32
D_EXPERT = 512
NEG = -1e30
EPS = 1e-6
LANES = 128
INT_MIN = -(2 ** 31)

SEG_QA, SEG_KVA, SEG_QI, SEG_KI, SEG_WI, SEG_QB, SEG_KB, SEG_VB = (
    (0, 512), (512, 1536), (1536, 2048), (2048, 2176), (2176, 2304), (2304, 2816), (2816, 3328), (3328, 3840))
PACKED_COLS = 3840

VMEM_LIMIT = 56 * 1024 * 1024
ATTN_TILE = 256
KEY_SUB = 128
EXPERT_ROWS = 256
MOE_CHUNK = 1024


def _cparams(sem):
    return pltpu.CompilerParams(dimension_semantics=sem, vmem_limit_bytes=VMEM_LIMIT)


def _dot(a, b):
    return jnp.dot(a, b, preferred_element_type=F32)


def _dot_nt(a, b):
    return lax.dot_general(a, b, (((1,), (1,)), ((), ())), preferred_element_type=F32)


def _rms(x, g):
    return x * lax.rsqrt(jnp.mean(x * x, axis=-1, keepdims=True) + EPS) * g


def _sortable_key(score):
    bits = lax.bitcast_convert_type(score, I32)
    key = jnp.where(bits < 0, bits ^ jnp.int32(0x7FFFFFFF), bits)
    return jnp.where(bits == jnp.int32(INT_MIN), 0, key)


def _proj_kernel(x_ref, g_ref, w_ref, gk_ref, qa_ref, kva_ref, ka16_ref, vat_ref, qi_ref, ki_ref, ki16_ref, wi_ref,
                 qb_ref, kb_ref, vb_ref, kb16_ref, vbt_ref):
    h = _rms(x_ref[...], g_ref[...]).astype(BF16)
    tt = vat_ref.shape[2]

    def seg(s):
        return _dot(h, w_ref[:, s[0]:s[1]])

    def store_transposed(dst_ref, v):
        for c in range(dst_ref.shape[0]):
            dst_ref[c] = v[c * tt:(c + 1) * tt, :].T.astype(BF16)

    qa_ref[...] = (seg(SEG_QA) * HEAD_DIM ** -0.5).astype(BF16)
    kva = seg(SEG_KVA)
    kva_ref[...] = kva
    ka16_ref[...] = kva[:, :A_WIDTH].astype(BF16)
    store_transposed(vat_ref, kva[:, A_WIDTH:])
    qi_ref[...] = (seg(SEG_QI) * D_IDX ** -0.5).astype(BF16)
    ki = _rms(seg(SEG_KI), gk_ref[...])
    ki_ref[...] = ki
    ki16_ref[...] = ki.astype(BF16)
    wi_ref[...] = seg(SEG_WI) * N_IDX_HEADS ** -0.5
    qb_ref[...] = (seg(SEG_QB) * HEAD_DIM ** -0.5).astype(BF16)
    kb = seg(SEG_KB)
    kb_ref[...] = kb
    kb16_ref[...] = kb.astype(BF16)
    vb = seg(SEG_VB)
    vb_ref[...] = vb
    store_transposed(vbt_ref, vb)


def _proj(x, g, w_packed, gk2):
    n, d = x.shape
    tm = min(n, 512)
    tt = min(ATTN_TILE, tm)
    vt = (A_WIDTH, None)
    widths = [(512, BF16), (1024, F32), (512, BF16), vt, (512, BF16), (128, F32), (128, BF16), (128, F32),
              (512, BF16), (512, F32), (512, F32), (512, BF16), vt]

    def spec(w, dt):
        if dt is None:
            return pl.BlockSpec((tm // tt, w, tt), lambda i: (i, 0, 0))
        return pl.BlockSpec((tm, w), lambda i: (i, 0))

    def shape(w, dt):
        if dt is None:
            return jax.ShapeDtypeStruct((n // tt, w, tt), BF16)
        return jax.ShapeDtypeStruct((n, w), dt)

    return pl.pallas_call(
        _proj_kernel,
        grid=(n // tm,),
        in_specs=[pl.BlockSpec((tm, d), lambda i: (i, 0)),
                  pl.BlockSpec((1, d), lambda i: (0, 0)),
                  pl.BlockSpec((d, PACKED_COLS), lambda i: (0, 0)),
                  pl.BlockSpec((1, LANES), lambda i: (0, 0))],
        out_specs=[spec(w, dt) for w, dt in widths],
        out_shape=[shape(w, dt) for w, dt in widths],
        compiler_params=_cparams(("arbitrary",)),
        name="proj",
    )(x, g, w_packed, gk2)


def _norm_matmul_kernel(x_ref, g_ref, w_ref, o_ref):
    o_ref[...] = _dot(_rms(x_ref[...], g_ref[...]).astype(BF16), w_ref[...])


def _norm_matmul(x, g, w16):
    n, d = x.shape
    tm = min(n, 512)
    return pl.pallas_call(
        _norm_matmul_kernel,
        grid=(n // tm,),
        in_specs=[pl.BlockSpec((tm, d), lambda i: (i, 0)),
                  pl.BlockSpec((1, d), lambda i: (0, 0)),
                  pl.BlockSpec(w16.shape, lambda i: (0, 0))],
        out_specs=pl.BlockSpec((tm, w16.shape[1]), lambda i: (i, 0)),
        out_shape=jax.ShapeDtypeStruct((n, w16.shape[1]), F32),
        compiler_params=_cparams(("arbitrary",)),
        name="norm_matmul",
    )(x, g, w16)


def _split_halves(q_ref, dst_ref, n_pairs):
    t = q_ref.shape[0]
    lo = lax.broadcasted_iota(I32, (t, LANES), 1) < HEAD_DIM
    for p in range(n_pairs):
        qp = q_ref[:, p * LANES:(p + 1) * LANES]
        dst_ref[2 * p] = jnp.where(lo, qp, jnp.zeros_like(qp))
        dst_ref[2 * p + 1] = jnp.where(lo, jnp.zeros_like(qp), qp)


def _online_softmax_step(s, h, m_ref, l_ref):
    m_prev = m_ref[h]
    m_new = jnp.maximum(m_prev, jnp.max(s, axis=1, keepdims=True))
    alpha = jnp.exp(m_prev - m_new)
    p = jnp.exp(s - jnp.concatenate([m_new] * (s.shape[1] // LANES), axis=1))
    l_ref[h] = alpha * l_ref[h] + jnp.sum(p, axis=1, keepdims=True)
    m_ref[h] = m_new
    return p, alpha


def _online_softmax_step_t(s, h, m_ref, l_ref):
    m_prev = m_ref[h]
    m_new = jnp.maximum(m_prev, jnp.max(s, axis=0, keepdims=True))
    alpha = jnp.exp(m_prev - m_new)
    p = jnp.exp(s - m_new)
    l_ref[h] = alpha * l_ref[h] + jnp.sum(p, axis=0, keepdims=True)
    m_ref[h] = m_new
    return p, alpha


def _t5_bucket(dist):
    n = jnp.maximum(dist, 0)
    nf = jnp.maximum(n, 1).astype(F32)
    large = MAX_EXACT + (jnp.log(nf / MAX_EXACT) / math.log(MAX_DISTANCE / MAX_EXACT)
                         * (N_BUCKETS - MAX_EXACT)).astype(I32)
    large = jnp.minimum(large, N_BUCKETS - 1)
    return jnp.where(n < MAX_EXACT, n, large)


def _bias_lookup(table, dist):
    onehot = jax.nn.one_hot(_t5_bucket(dist), N_BUCKETS, dtype=F32)
    return jnp.dot(onehot, table.astype(F32), precision=lax.Precision.HIGHEST)


def _prompt_bias_tiles(table, t):
    assert t >= MAX_DISTANCE
    r = jnp.arange(t, dtype=I32)[None, :]
    c = jnp.arange(t, dtype=I32)[:, None]
    tiles = []
    for delta in range(3):
        dist = delta * t + r - c
        b = jnp.transpose(_bias_lookup(table, dist), (2, 0, 1))
        tiles.append(jnp.where(dist[None] >= 0, b, NEG))
    return jnp.stack(tiles)


def _dsa_prompt_kernel(k_sel, n_bits, qa_ref, qi_ref, wi_ref, k2_ref, k_ref, vt_ref, bt_ref, o_ref,
                       qah_ref, qih_ref, key_ref, m_ref, l_ref, acc_ref):
    t = qa_ref.shape[0]
    i = pl.program_id(1)
    nj = i + 1
    kidx = lax.broadcasted_iota(I32, (t, t), 0)
    qidx = lax.broadcasted_iota(I32, (t, t), 1)

    _split_halves(qa_ref, qah_ref, N_HEADS_A // 2)
    _split_halves(qi_ref, qih_ref, N_IDX_HEADS // 2)
    wi_t = wi_ref[...].T

    def score_chunk(j, carry):
        k0 = pl.multiple_of(j * t, t)
        for ks in range(0, t, KEY_SUB):
            keys = slice(ks, ks + KEY_SUB)
            kc = k2_ref[pl.ds(k0 + ks, KEY_SUB), :]
            acc = jnp.zeros((KEY_SUB, t), F32)
            for h in range(N_IDX_HEADS):
                acc = acc + jnp.maximum(_dot_nt(kc, qih_ref[h]), 0.0) * wi_t[h:h + 1, :]
            key = _sortable_key(acc)
            kpos = k0 + ks + lax.broadcasted_iota(I32, (KEY_SUB, t), 0)
            qpos = i * t + lax.broadcasted_iota(I32, (KEY_SUB, t), 1)
            key_ref[j, keys, :] = jnp.where(kpos <= qpos, key, jnp.int32(INT_MIN))
        return carry

    lax.fori_loop(0, nj, score_chunk, 0)

    def count_where(pred_fn):
        def body(j, c):
            hit = jnp.where(pred_fn(key_ref[j], j), 1.0, 0.0)
            return c + jnp.sum(hit.reshape(t // 8, 8, t), axis=0)
        return jnp.sum(lax.fori_loop(0, nj, body, jnp.zeros((8, t), F32)), axis=0, keepdims=True)

    k_f = float(k_sel)
    n_valid = (i * t + lax.broadcasted_iota(I32, (1, t), 1) + 1).astype(F32)

    def search_cond(state):
        b, _, cnt = state
        return (b < 32) & (jnp.max(cnt) > k_f)

    def search_step(state):
        b, tu, cnt = state
        cand_u = tu | lax.shift_left(jnp.int32(1), 31 - b)
        cand = cand_u ^ jnp.int32(INT_MIN)
        c = count_where(lambda kc, j: kc >= cand)
        take = c >= k_f
        return b + 1, jnp.where(take, cand_u, tu), jnp.where(take, c, cnt)

    _, tu, cnt = lax.while_loop(search_cond, lambda state: search_step(search_step(state)),
                                (jnp.int32(0), jnp.zeros((1, t), I32), n_valid))
    thr = tu ^ jnp.int32(INT_MIN)

    def tie_search(_):
        need = k_f - count_where(lambda kc, j: kc > thr)

        def cut_step(b, c0):
            cand = c0 | lax.shift_left(jnp.int32(1), n_bits - 1 - b)
            c = count_where(lambda kc, j: (kc == thr) & (j * t + kidx < cand))
            return jnp.where(c < need, cand, c0)
        return lax.fori_loop(0, n_bits, cut_step, jnp.zeros((1, t), I32))

    cut = lax.cond(jnp.max(cnt) > k_f, tie_search, lambda _: jnp.full((1, t), 2 ** 30, I32), 0)

    m_ref[...] = jnp.full(m_ref.shape, NEG, F32)
    l_ref[...] = jnp.zeros(l_ref.shape, F32)
    acc_ref[...] = jnp.zeros(acc_ref.shape, F32)

    def attn_chunk(j, carry):
        k0 = pl.multiple_of(j * t, t)
        kc = key_ref[j]
        sel = (kc > thr) | ((kc == thr) & (k0 + kidx <= cut))
        dsel = jnp.minimum(i - j, 2)
        for ks in range(0, t, KEY_SUB):
            keys = slice(ks, ks + KEY_SUB)
            for p in range(N_HEADS_A // 2):
                kp = k_ref[pl.ds(k0 + ks, KEY_SUB), p * LANES:(p + 1) * LANES]
                vp_t = vt_ref[j, p * LANES:(p + 1) * LANES, keys]
                for half in range(2):
                    h = 2 * p + half
                    s = jnp.where(sel[keys], _dot_nt(kp, qah_ref[h]) + bt_ref[dsel, h, keys, :], NEG)
                    pr, alpha = _online_softmax_step_t(s, h, m_ref, l_ref)
                    pv = _dot(vp_t, pr.astype(BF16))
                    rows = slice(half * HEAD_DIM, (half + 1) * HEAD_DIM)
                    acc_ref[p, rows, :] = acc_ref[p, rows, :] * alpha + pv[rows]
        return carry

    lax.fori_loop(0, nj, attn_chunk, 0)

    for p in range(N_HEADS_A // 2):
        o_t = jnp.concatenate([acc_ref[p, :HEAD_DIM, :] / l_ref[2 * p],
                               acc_ref[p, HEAD_DIM:, :] / l_ref[2 * p + 1]], axis=0)
        o_ref[:, p * LANES:(p + 1) * LANES] = o_t.T.astype(BF16)


def _dsa_prompt(qa, qi, wi, ki16, ka16, vat, bias_tiles, batch, seq):
    t = min(ATTN_TILE, seq)
    nq = seq // t
    k_sel = min(TOPK_MAX, seq // 4)
    n_bits = max(1, (seq - 1).bit_length())
    kern = functools.partial(_dsa_prompt_kernel, k_sel, n_bits)
    return pl.pallas_call(
        kern,
        grid=(batch, nq),
        in_specs=[pl.BlockSpec((t, A_WIDTH), lambda b, i: (b * nq + i, 0)),
                  pl.BlockSpec((t, A_WIDTH), lambda b, i: (b * nq + i, 0)),
                  pl.BlockSpec((t, LANES), lambda b, i: (b * nq + i, 0)),
                  pl.BlockSpec((seq, LANES), lambda b, i: (b, 0)),
                  pl.BlockSpec((seq, A_WIDTH), lambda b, i: (b, 0)),
                  pl.BlockSpec((nq, A_WIDTH, t), lambda b, i: (b, 0, 0)),
                  pl.BlockSpec(bias_tiles.shape, lambda b, i: (0, 0, 0, 0))],
        out_specs=pl.BlockSpec((t, A_WIDTH), lambda b, i: (b * nq + i, 0)),
        out_shape=jax.ShapeDtypeStruct((batch * seq, A_WIDTH), BF16),
        scratch_shapes=[pltpu.VMEM((N_HEADS_A, t, LANES), BF16),
                        pltpu.VMEM((N_IDX_HEADS, t, LANES), BF16),
                        pltpu.VMEM((nq, t, t), I32),
                        pltpu.VMEM((N_HEADS_A, 1, t), F32),
                        pltpu.VMEM((N_HEADS_A, 1, t), F32),
                        pltpu.VMEM((N_HEADS_A // 2, LANES, t), F32)],
        compiler_params=_cparams(("arbitrary", "arbitrary")),
        name="dsa_prompt",
    )(qa, qi, wi, ki16, ka16, vat, bias_tiles)


def _subln(o0, o1, lam, lam_init, g):
    d = o0 - lam * o1
    return _rms(d, g) * (1.0 - lam_init)


def _diff_prompt_kernel(lam_init, lam_ref, qb_ref, kb_ref, vbt_ref, bt_ref, g_ref, o_ref,
                        qh_ref, m_ref, l_ref, acc_ref):
    t = qb_ref.shape[0]
    i = pl.program_id(1)
    _split_halves(qb_ref, qh_ref, N_HEADS_B)
    m_ref[...] = jnp.full(m_ref.shape, NEG, F32)
    l_ref[...] = jnp.zeros(l_ref.shape, F32)
    acc_ref[...] = jnp.zeros(acc_ref.shape, F32)

    def chunk(j, carry):
        k0 = pl.multiple_of(j * t, t)
        dsel = jnp.minimum(i - j, 2)
        for ks in range(0, t, KEY_SUB):
            keys = slice(ks, ks + KEY_SUB)
            for h in range(N_HEADS_B):
                kp = kb_ref[pl.ds(k0 + ks, KEY_SUB), h * LANES:(h + 1) * LANES]
                vp_t = vbt_ref[j, h * LANES:(h + 1) * LANES, keys]
                for mp in range(2):
                    c = 2 * h + mp
                    s = _dot_nt(kp, qh_ref[c]) + bt_ref[dsel, c, keys, :]
                    pr, alpha = _online_softmax_step_t(s, c, m_ref, l_ref)
                    acc_ref[c] = acc_ref[c] * alpha + _dot(vp_t, pr.astype(BF16))
        return carry

    lax.fori_loop(0, i + 1, chunk, 0)

    lam = lam_ref[0]
    for h in range(N_HEADS_B):
        o0 = (acc_ref[2 * h] / l_ref[2 * h]).T
        o1 = (acc_ref[2 * h + 1] / l_ref[2 * h + 1]).T
        o_ref[:, h * LANES:(h + 1) * LANES] = _subln(o0, o1, lam, lam_init, g_ref[...]).astype(BF16)


def _diff_prompt(lam, lam_init, qb, kb16, vbt, bias_tiles, g_sub, batch, seq):
    t = min(ATTN_TILE, seq)
    nq = seq // t
    n_maps = 2 * N_HEADS_B
    kern = functools.partial(_diff_prompt_kernel, lam_init)
    return pl.pallas_call(
        kern,
        grid=(batch, nq),
        in_specs=[pl.BlockSpec(memory_space=pltpu.SMEM),
                  pl.BlockSpec((t, B_WIDTH), lambda b, i: (b * nq + i, 0)),
                  pl.BlockSpec((seq, B_WIDTH), lambda b, i: (b, 0)),
                  pl.BlockSpec((nq, B_WIDTH, t), lambda b, i: (b, 0, 0)),
                  pl.BlockSpec(bias_tiles.shape, lambda b, i: (0, 0, 0, 0)),
                  pl.BlockSpec((1, LANES), lambda b, i: (0, 0))],
        out_specs=pl.BlockSpec((t, B_WIDTH), lambda b, i: (b * nq + i, 0)),
        out_shape=jax.ShapeDtypeStruct((batch * seq, B_WIDTH), BF16),
        scratch_shapes=[pltpu.VMEM((n_maps, t, LANES), BF16),
                        pltpu.VMEM((n_maps, 1, t), F32),
                        pltpu.VMEM((n_maps, 1, t), F32),
                        pltpu.VMEM((n_maps, LANES, t), F32)],
        compiler_params=_cparams(("arbitrary", "arbitrary")),
        name="diff_prompt",
    )(lam, qb, kb16, vbt, bias_tiles, g_sub)


SAMPLE_ROWS = 8


def _sample_scores_kernel(n_pages, g_pages, dec_seq, pt_ref, *refs):
    page_refs = refs[:g_pages]
    qx_ref, wx_ref, knew_ref, key_ref = refs[g_pages:]
    del pt_ref
    step = pl.program_id(1)
    n_steps = n_pages // g_pages
    r8 = SAMPLE_ROWS

    def chunk_scores(kt16):
        d = jnp.maximum(_dot(qx_ref[0], kt16), 0.0) * wx_ref[0]
        acc = d[0:r8]
        for h in range(1, N_IDX_HEADS):
            acc = acc + d[h * r8:(h + 1) * r8]
        return _sortable_key(acc)

    for g in range(g_pages):
        key_ref[0, step * g_pages + g] = chunk_scores(page_refs[g][0].astype(BF16))

    @pl.when(step == n_steps - 1)
    def _():
        row = lax.broadcasted_iota(I32, (r8, LANES), 0)
        lane = lax.broadcasted_iota(I32, (r8, LANES), 1)
        key_new = chunk_scores(knew_ref[0])
        key_ref[0, n_pages] = jnp.where((lane <= row) & (lane < dec_seq), key_new, jnp.int32(INT_MIN))


def _sample_scores(page_table, cache_kidx_t, qx, wx, knew_t, dec_seq, g_pages):
    batch, n_pages = page_table.shape
    n_steps = n_pages // g_pages
    rows = N_IDX_HEADS * SAMPLE_ROWS
    kern = functools.partial(_sample_scores_kernel, n_pages, g_pages, dec_seq)
    page_specs = [pl.BlockSpec((1, D_IDX, LANES), functools.partial(
        lambda b, s, pt, g: (pt[b, s * g_pages + g], 0, 0), g=g)) for g in range(g_pages)]
    grid_spec = pltpu.PrefetchScalarGridSpec(
        num_scalar_prefetch=1,
        grid=(batch, n_steps),
        in_specs=page_specs + [pl.BlockSpec((1, rows, D_IDX), lambda b, s, pt: (b, 0, 0)),
                               pl.BlockSpec((1, rows, LANES), lambda b, s, pt: (b, 0, 0)),
                               pl.BlockSpec((1, D_IDX, LANES), lambda b, s, pt: (b, 0, 0))],
        out_specs=pl.BlockSpec((1, n_pages + 1, SAMPLE_ROWS, LANES), lambda b, s, pt: (b, 0, 0, 0)))
    return pl.pallas_call(
        kern,
        grid_spec=grid_spec,
        out_shape=jax.ShapeDtypeStruct((batch, n_pages + 1, SAMPLE_ROWS, LANES), I32),
        compiler_params=_cparams(("arbitrary", "arbitrary")),
        name="sample_scores",
    )(page_table, *([cache_kidx_t] * g_pages), qx, wx, knew_t)


def _sample_select_kernel(k_sel, n_bits, key_ref, mask_ref):
    shape = key_ref.shape
    colg = lax.broadcasted_iota(I32, shape, 1) * LANES + lax.broadcasted_iota(I32, shape, 3)

    def count(pred):
        c = jnp.sum(jnp.where(pred, 1.0, 0.0), axis=1, keepdims=True)
        return jnp.sum(c, axis=3, keepdims=True)

    stat = (shape[0], 1, shape[2], 1)

    def bit_step(b, tu):
        cand_u = tu | lax.shift_left(jnp.int32(1), 31 - b)
        cand = cand_u ^ jnp.int32(INT_MIN)
        return jnp.where(count(key_ref[...] >= cand) >= k_sel, cand_u, tu)

    tu = lax.fori_loop(0, 32, bit_step, jnp.zeros(stat, I32))
    thr = tu ^ jnp.int32(INT_MIN)
    need = k_sel - count(key_ref[...] > thr)

    def cut_step(b, c0):
        cand = c0 | lax.shift_left(jnp.int32(1), n_bits - 1 - b)
        return jnp.where(count((key_ref[...] == thr) & (colg < cand)) < need, cand, c0)

    cut = lax.fori_loop(0, n_bits, cut_step, jnp.zeros(stat, I32))
    keys = key_ref[...]
    sel = (keys > thr) | ((keys == thr) & (colg <= cut))
    sel = sel & (keys != jnp.int32(INT_MIN))
    mask_ref[...] = jnp.where(sel, 0.0, NEG)


def _sample_select(keys, k_sel):
    batch, n_chunks = keys.shape[:2]
    bt = 8 if batch % 8 == 0 else 1
    n_bits = max(1, (n_chunks * LANES - 1).bit_length())
    blk = (bt, n_chunks, SAMPLE_ROWS, LANES)
    return pl.pallas_call(
        functools.partial(_sample_select_kernel, k_sel, n_bits),
        grid=(batch // bt,),
        in_specs=[pl.BlockSpec(blk, lambda i: (i, 0, 0, 0))],
        out_specs=pl.BlockSpec(blk, lambda i: (i, 0, 0, 0)),
        out_shape=jax.ShapeDtypeStruct(keys.shape, F32),
        compiler_params=_cparams(("arbitrary",)),
        name="sample_select",
    )(keys)


def _sample_attn_kernel(lam_init, n_pages, g_pages, pt_ref, lam_ref, *refs):
    kva_refs = refs[:g_pages]
    kb_refs = refs[g_pages:2 * g_pages]
    vb_refs = refs[2 * g_pages:3 * g_pages]
    (qa_ref, qb_ref, mask_ref, maskn_ref, ba_ref, bb_ref, kvan_ref, kbn_ref, vbn_ref, g_ref,
     o_ref, m_ref, l_ref, acca_ref, accb_ref) = refs[3 * g_pages:]
    del pt_ref
    step = pl.program_id(1)
    n_steps = n_pages // g_pages
    r8 = SAMPLE_ROWS

    @pl.when(step == 0)
    def _():
        m_ref[...] = jnp.full(m_ref.shape, NEG, F32)
        l_ref[...] = jnp.zeros(l_ref.shape, F32)
        acca_ref[...] = jnp.zeros(acca_ref.shape, F32)
        accb_ref[...] = jnp.zeros(accb_ref.shape, F32)

    def both(kva, kb, vb, mask8, bsel):
        n = len(kva)
        cols = [slice(g * LANES, (g + 1) * LANES) for g in range(n)]
        s = jnp.concatenate(
            [_dot(qa_ref[0], kva[g][0, :A_WIDTH, :].astype(BF16))
             + (ba_ref[bsel[g]] + jnp.concatenate([mask8[g]] * N_HEADS_A, axis=0)) for g in range(n)], axis=1)
        pr, alpha = _online_softmax_step(s, 0, m_ref, l_ref)
        pr = pr.astype(BF16)
        pv = _dot_nt(pr[:, cols[0]], kva[0][0, A_WIDTH:, :].astype(BF16))
        for g in range(1, n):
            pv = pv + _dot_nt(pr[:, cols[g]], kva[g][0, A_WIDTH:, :].astype(BF16))
        acca_ref[...] = acca_ref[...] * jnp.concatenate([alpha] * (A_WIDTH // LANES), axis=1) + pv
        s = jnp.concatenate([_dot(qb_ref[0], kb[g][0].astype(BF16)) + bb_ref[bsel[g]] for g in range(n)], axis=1)
        pr, alpha = _online_softmax_step(s, 1, m_ref, l_ref)
        pr = pr.astype(BF16)
        pv = []
        for h in range(N_HEADS_B):
            rows = slice(2 * r8 * h, 2 * r8 * (h + 1))
            acc = None
            for g in range(n):
                d = _dot(pr[rows, cols[g]], vb[g][0, pl.ds(h, LANES, stride=N_HEADS_B), :].astype(BF16))
                acc = d if acc is None else acc + d
            pv.append(acc)
        accb_ref[...] = accb_ref[...] * alpha + jnp.concatenate(pv, axis=0)

    bsel = [0] * (g_pages - 1) + [jnp.where(step == n_steps - 1, 1, 0)]
    both(kva_refs, kb_refs, vb_refs, [mask_ref[0, g] for g in range(g_pages)], bsel)

    @pl.when(step == n_steps - 1)
    def _():
        both([kvan_ref], [kbn_ref], [vbn_ref], [maskn_ref[0, 0]], [2])
        lane = lax.broadcasted_iota(I32, (r8, A_WIDTH), 1)
        oa = jnp.zeros((r8, A_WIDTH), F32)
        la = jnp.concatenate([l_ref[0]] * (A_WIDTH // LANES), axis=1)
        acca = acca_ref[...] / la
        for h in range(N_HEADS_A):
            blk = acca[h * r8:(h + 1) * r8]
            oa = jnp.where((lane >= h * HEAD_DIM) & (lane < (h + 1) * HEAD_DIM), blk, oa)
        o_ref[0, :, :A_WIDTH] = oa.astype(BF16)
        lam = lam_ref[0]
        ob = accb_ref[...] / l_ref[1]
        for h in range(N_HEADS_B):
            r0 = 2 * h * r8
            o_ref[0, :, A_WIDTH + h * LANES:A_WIDTH + (h + 1) * LANES] = _subln(
                ob[r0:r0 + r8], ob[r0 + r8:r0 + 2 * r8], lam, lam_init, g_ref[...]).astype(BF16)


def _sample_attn(lam, lam_init, page_table, cache_kva_t, cache_kb_t, cache_vb, qa_x, qb_x, mask, bias_a, bias_b,
                 kva_new_t, kb_new_t, vb_new, g_sub, g_pages):
    batch, n_pages = page_table.shape
    n_steps = n_pages // g_pages
    rows = N_HEADS_A * SAMPLE_ROWS
    kern = functools.partial(_sample_attn_kernel, lam_init, n_pages, g_pages)

    def page_spec(width, g):
        return pl.BlockSpec((1, width, LANES), lambda b, s, pt: (pt[b, s * g_pages + g], 0, 0))

    in_specs = ([pl.BlockSpec(memory_space=pltpu.SMEM)]
                + [page_spec(2 * A_WIDTH, g) for g in range(g_pages)]
                + [page_spec(B_WIDTH, g) for g in range(g_pages)]
                + [page_spec(B_WIDTH, g) for g in range(g_pages)]
                + [pl.BlockSpec((1, rows, A_WIDTH), lambda b, s, pt: (b, 0, 0)),
                   pl.BlockSpec((1, rows, B_WIDTH), lambda b, s, pt: (b, 0, 0)),
                   pl.BlockSpec((1, g_pages, SAMPLE_ROWS, LANES), lambda b, s, pt: (b, s, 0, 0)),
                   pl.BlockSpec((1, 1, SAMPLE_ROWS, LANES), lambda b, s, pt: (b, n_pages, 0, 0)),
                   pl.BlockSpec(bias_a.shape, lambda b, s, pt: (0, 0, 0)),
                   pl.BlockSpec(bias_b.shape, lambda b, s, pt: (0, 0, 0)),
                   pl.BlockSpec((1, 2 * A_WIDTH, LANES), lambda b, s, pt: (b, 0, 0)),
                   pl.BlockSpec((1, B_WIDTH, LANES), lambda b, s, pt: (b, 0, 0)),
                   pl.BlockSpec((1, B_WIDTH, LANES), lambda b, s, pt: (b, 0, 0)),
                   pl.BlockSpec((1, LANES), lambda b, s, pt: (0, 0))])
    grid_spec = pltpu.PrefetchScalarGridSpec(
        num_scalar_prefetch=1,
        grid=(batch, n_steps),
        in_specs=in_specs,
        out_specs=pl.BlockSpec((1, SAMPLE_ROWS, A_WIDTH + B_WIDTH), lambda b, s, pt: (b, 0, 0)),
        scratch_shapes=[pltpu.VMEM((2, rows, LANES), F32),
                        pltpu.VMEM((2, rows, LANES), F32),
                        pltpu.VMEM((rows, A_WIDTH), F32),
                        pltpu.VMEM((rows, LANES), F32)])
    return pl.pallas_call(
        kern,
        grid_spec=grid_spec,
        out_shape=jax.ShapeDtypeStruct((batch, SAMPLE_ROWS, A_WIDTH + B_WIDTH), BF16),
        compiler_params=_cparams(("arbitrary", "arbitrary")),
        name="sample_attn",
    )(page_table, lam, *([cache_kva_t] * g_pages), *([cache_kb_t] * g_pages), *([cache_vb] * g_pages),
      qa_x, qb_x, mask, mask, bias_a, bias_b, kva_new_t, kb_new_t, vb_new, g_sub)


def _sample_bias(table, past, dec_seq, causal_new):
    t = jnp.arange(SAMPLE_ROWS, dtype=I32)[:, None]
    c = jnp.arange(LANES, dtype=I32)[None, :]
    far = jnp.full((SAMPLE_ROWS, LANES), MAX_DISTANCE, I32)
    last = t + LANES - c
    new = t - c
    out = []
    for dist in (far, last, new):
        b = jnp.transpose(_bias_lookup(table, dist), (2, 0, 1))
        out.append(b.reshape(table.shape[1] * SAMPLE_ROWS, LANES))
    if causal_new:
        ok = jnp.tile((new >= 0) & (c < dec_seq), (table.shape[1], 1))
        out[2] = jnp.where(ok, out[2], NEG)
    del past
    return jnp.stack(out)


def _block_diag_rows(q):
    b, t = q.shape[:2]
    q = jnp.pad(q, ((0, 0), (0, SAMPLE_ROWS - t), (0, 0), (0, 0)))
    q = jnp.transpose(q, (0, 2, 1, 3))
    eye = jnp.eye(8, dtype=q.dtype)
    return jnp.einsum('bctd,cg->bctgd', q, eye).reshape(b, 8 * SAMPLE_ROWS, 8 * HEAD_DIM)


def _tail1_kernel(n_mix, x_ref, *refs):
    o_refs = refs[:n_mix]
    w_refs = refs[n_mix:2 * n_mix]
    g_ref, wq_ref, x1_ref, qm_ref = refs[2 * n_mix:]
    x1 = x_ref[...]
    for o_ref, w_ref in zip(o_refs, w_refs):
        x1 = x1 + _dot(o_ref[...], w_ref[...])
    x1_ref[...] = x1
    hm = _rms(x1, g_ref[...]).astype(BF16)
    qm_ref[...] = (_dot(hm, wq_ref[...]) * HEAD_DIM ** -0.5).astype(BF16)


def _tail1(x, mixes, g_mem, wq16):
    n, d = x.shape
    tm = min(n, 512)
    n_mix = len(mixes)
    mw = wq16.shape[1]
    return pl.pallas_call(
        functools.partial(_tail1_kernel, n_mix),
        grid=(n // tm,),
        in_specs=([pl.BlockSpec((tm, d), lambda i: (i, 0))]
                  + [pl.BlockSpec((tm, o.shape[1]), lambda i: (i, 0)) for o, _ in mixes]
                  + [pl.BlockSpec(w.shape, lambda i: (0, 0)) for _, w in mixes]
                  + [pl.BlockSpec((1, d), lambda i: (0, 0)),
                     pl.BlockSpec(wq16.shape, lambda i: (0, 0))]),
        out_specs=[pl.BlockSpec((tm, d), lambda i: (i, 0)), pl.BlockSpec((tm, mw), lambda i: (i, 0))],
        out_shape=[jax.ShapeDtypeStruct((n, d), F32), jax.ShapeDtypeStruct((n, mw), BF16)],
        compiler_params=_cparams(("arbitrary",)),
        name="tail1",
    )(x, *[o for o, _ in mixes], *[w for _, w in mixes], g_mem, wq16)


def _tail2_kernel(qm_ref, mkv_ref, x1_ref, wo_ref, g_ref, wr_ref, br_ref, x2_ref, h_ref, route_ref):
    tq = qm_ref.shape[1]
    mem_w = N_HEADS_MEM * HEAD_DIM
    lo = lax.broadcasted_iota(I32, (tq, LANES), 1) < HEAD_DIM
    mkv = mkv_ref[0].astype(BF16)
    outs = []
    for p in range(N_HEADS_MEM // 2):
        qp = qm_ref[0, :, p * LANES:(p + 1) * LANES]
        kp = mkv[:, p * LANES:(p + 1) * LANES]
        vp = mkv[:, mem_w + p * LANES:mem_w + (p + 1) * LANES]
        pv = []
        for half in range(2):
            qh = jnp.where(lo == (half == 0), qp, jnp.zeros_like(qp))
            s = _dot_nt(qh, kp)
            e = jnp.exp(s - jnp.max(s, axis=1, keepdims=True))
            pv.append(_dot(e.astype(BF16), vp) / jnp.sum(e, axis=1, keepdims=True))
        outs.append(jnp.where(lo, pv[0], pv[1]))
    om = jnp.concatenate(outs, axis=1).astype(BF16)
    x2 = x1_ref[0] + _dot(om, wo_ref[...])
    x2_ref[0] = x2
    h = _rms(x2, g_ref[...])
    h_ref[0] = h.astype(BF16)

    lg = jnp.dot(h, wr_ref[...], preferred_element_type=F32, precision=lax.Precision.HIGHEST) + br_ref[...]
    lane = lax.broadcasted_iota(I32, (tq, LANES), 1)
    lane_f = lane.astype(F32)
    big = jnp.float32(1e9)
    is_g = lane < N_GROUPS
    m1 = jnp.max(jnp.where(is_g, lg, -jnp.inf), axis=1, keepdims=True)
    grp = jnp.min(jnp.where(is_g & (lg == m1), lane_f, big), axis=1, keepdims=True)
    p_grp = 1.0 / jnp.sum(jnp.where(is_g, jnp.exp(lg - m1), 0.0), axis=1, keepdims=True)
    e0 = N_GROUPS + grp * EXPERTS_PER_GROUP
    in_g = (lane_f >= e0) & (lane_f < e0 + EXPERTS_PER_GROUP)
    v1 = jnp.max(jnp.where(in_g, lg, -jnp.inf), axis=1, keepdims=True)
    i1 = jnp.min(jnp.where(in_g & (lg == v1), lane_f, big), axis=1, keepdims=True)
    rest = in_g & (lane_f != i1)
    v2 = jnp.max(jnp.where(rest, lg, -jnp.inf), axis=1, keepdims=True)
    i2 = jnp.min(jnp.where(rest & (lg == v2), lane_f, big), axis=1, keepdims=True)
    e2 = jnp.exp(v2 - v1)
    den = 1.0 + e2
    g1 = (1.0 / den) * p_grp
    g2 = (e2 / den) * p_grp
    route = jnp.where(lane == 0, i1 - N_GROUPS, 0.0)
    route = jnp.where(lane == 1, i2 - N_GROUPS, route)
    route = jnp.where(lane == 2, g1, route)
    route = jnp.where(lane == 3, g2, route)
    route_ref[0] = route


def _tail2(qm, mkv, x1, wo16, g_ffn, wr, br):
    b, t, d = x1.shape
    tq = min(t, 512)
    mem_w = qm.shape[2]
    n_mem = mkv.shape[1]
    return pl.pallas_call(
        _tail2_kernel,
        grid=(b, t // tq),
        in_specs=[pl.BlockSpec((1, tq, mem_w), lambda i, j: (i, j, 0)),
                  pl.BlockSpec((1, n_mem, 2 * mem_w), lambda i, j: (i, 0, 0)),
                  pl.BlockSpec((1, tq, d), lambda i, j: (i, j, 0)),
                  pl.BlockSpec(wo16.shape, lambda i, j: (0, 0)),
                  pl.BlockSpec((1, d), lambda i, j: (0, 0)),
                  pl.BlockSpec(wr.shape, lambda i, j: (0, 0)),
                  pl.BlockSpec((1, LANES), lambda i, j: (0, 0))],
        out_specs=[pl.BlockSpec((1, tq, d), lambda i, j: (i, j, 0)),
                   pl.BlockSpec((1, tq, d), lambda i, j: (i, j, 0)),
                   pl.BlockSpec((1, tq, LANES), lambda i, j: (i, j, 0))],
        out_shape=[jax.ShapeDtypeStruct((b, t, d), F32), jax.ShapeDtypeStruct((b, t, d), BF16),
                   jax.ShapeDtypeStruct((b, t, LANES), F32)],
        compiler_params=_cparams(("arbitrary", "arbitrary")),
        name="tail2",
    )(qm, mkv, x1, wo16, g_ffn, wr, br)


FLAG_FIRST, FLAG_LAST, FLAG_VALID = 1, 2, 4


def _moe_ffn_kernel(pb_ref, pc_ref, fl_ref, be_ref, h_ref, rt_ref, gate_ref, wgu_ref, wdn_ref, y_ref, acc_ref):
    del pb_ref, be_ref
    p = pl.program_id(0)
    fl = fl_ref[p]
    c = h_ref.shape[0]
    r = acc_ref.shape[0]

    @pl.when((fl & FLAG_FIRST) != 0)
    def _():
        acc_ref[...] = jnp.zeros(acc_ref.shape, F32)

    @pl.when((fl & FLAG_VALID) != 0)
    def _():
        rel = rt_ref[...] - pc_ref[p] * c
        onehot = jnp.where(rel == lax.broadcasted_iota(I32, (r, c), 1), 1.0, 0.0).astype(BF16)
        acc_ref[...] += _dot(onehot, h_ref[...])

    @pl.when((fl & FLAG_LAST) != 0)
    def _():
        gu = _dot(acc_ref[...].astype(BF16), wgu_ref[0])
        gate = gu[:, :D_EXPERT]
        up = gu[:, D_EXPERT:]
        act = gate * (1.0 / (1.0 + jnp.exp(-gate))) * up
        y_ref[...] = (_dot(act.astype(BF16), wdn_ref[0]) * gate_ref[...]).astype(BF16)


def _moe_ffn(pb, pc, flags, block_expert, h16, row_tok, row_gate, wgu16, wdn16, chunk):
    n_rows = row_tok.shape[0]
    d = h16.shape[1]
    r = EXPERT_ROWS
    grid_spec = pltpu.PrefetchScalarGridSpec(
        num_scalar_prefetch=4,
        grid=(pb.shape[0],),
        in_specs=[pl.BlockSpec((chunk, d), lambda p, pb, pc, fl, be: (pc[p], 0)),
                  pl.BlockSpec((r, 1), lambda p, pb, pc, fl, be: (pb[p], 0)),
                  pl.BlockSpec((r, 1), lambda p, pb, pc, fl, be: (pb[p], 0)),
                  pl.BlockSpec((1,) + wgu16.shape[1:], lambda p, pb, pc, fl, be: (be[pb[p]], 0, 0)),
                  pl.BlockSpec((1,) + wdn16.shape[1:], lambda p, pb, pc, fl, be: (be[pb[p]], 0, 0))],
        out_specs=pl.BlockSpec((r, d), lambda p, pb, pc, fl, be: (pb[p], 0)),
        scratch_shapes=[pltpu.VMEM((r, d), F32)])
    return pl.pallas_call(
        _moe_ffn_kernel,
        grid_spec=grid_spec,
        out_shape=jax.ShapeDtypeStruct((n_rows, d), BF16),
        compiler_params=_cparams(("arbitrary",)),
        name="moe_ffn",
    )(pb, pc, flags, block_expert, h16, row_tok.reshape(n_rows, 1), row_gate.reshape(n_rows, 1), wgu16, wdn16)


def _moe_combine_kernel(pb_ref, pc_ref, fl_ref, y_ref, rt_ref, x2_ref, g_ref, o_ref, acc_ref):
    del pb_ref
    p = pl.program_id(0)
    fl = fl_ref[p]
    c = acc_ref.shape[0]
    r = y_ref.shape[0]

    @pl.when((fl & FLAG_FIRST) != 0)
    def _():
        acc_ref[...] = jnp.zeros(acc_ref.shape, F32)

    @pl.when((fl & FLAG_VALID) != 0)
    def _():
        rel = rt_ref[0] - pc_ref[p] * c
        onehot_t = jnp.where(rel == lax.broadcasted_iota(I32, (c, r), 0), 1.0, 0.0).astype(BF16)
        acc_ref[...] += _dot(onehot_t, y_ref[...])

    @pl.when((fl & FLAG_LAST) != 0)
    def _():
        o_ref[...] = _rms(x2_ref[...] + acc_ref[...], g_ref[...])


def _moe_combine(pb, pc, flags, y_rows, row_tok, x2, g_final, chunk):
    n, d = x2.shape
    r = EXPERT_ROWS
    n_blocks = y_rows.shape[0] // r
    grid_spec = pltpu.PrefetchScalarGridSpec(
        num_scalar_prefetch=3,
        grid=(pb.shape[0],),
        in_specs=[pl.BlockSpec((r, d), lambda p, pb, pc, fl: (pb[p], 0)),
                  pl.BlockSpec((1, 1, r), lambda p, pb, pc, fl: (pb[p], 0, 0)),
                  pl.BlockSpec((chunk, d), lambda p, pb, pc, fl: (pc[p], 0)),
                  pl.BlockSpec((1, d), lambda p, pb, pc, fl: (0, 0))],
        out_specs=pl.BlockSpec((chunk, d), lambda p, pb, pc, fl: (pc[p], 0)),
        scratch_shapes=[pltpu.VMEM((chunk, d), F32)])
    return pl.pallas_call(
        _moe_combine_kernel,
        grid_spec=grid_spec,
        out_shape=jax.ShapeDtypeStruct((n, d), F32),
        compiler_params=_cparams(("arbitrary",)),
        name="moe_combine",
    )(pb, pc, flags, y_rows, row_tok.reshape(n_blocks, 1, r), x2, g_final)


def _moe_and_final(x2, h16, route, wgu16, wdn16, g_final):
    n, d = x2.shape
    n_assign = 2 * n
    experts = route[:, :2].astype(I32).reshape(n_assign)
    onehot = (experts[:, None] == jnp.arange(N_EXPERTS, dtype=I32)[None, :]).astype(I32)
    running = jnp.cumsum(onehot, axis=0)
    rank = jnp.sum((running - onehot) * onehot, axis=1)
    counts = running[-1]
    padded = (counts + EXPERT_ROWS - 1) // EXPERT_ROWS * EXPERT_ROWS
    pad_ends = jnp.cumsum(padded)
    pad_starts = pad_ends - padded
    pos = jnp.sum(onehot * pad_starts[None, :], axis=1) + rank
    n_blocks = -(-n_assign // EXPERT_ROWS) + N_EXPERTS
    n_rows = n_blocks * EXPERT_ROWS
    tok = jnp.arange(n_assign, dtype=I32) // 2
    gate_bits = lax.bitcast_convert_type(route[:, 2:4].reshape(n_assign), I32)
    rows = jnp.concatenate([jnp.full((n_rows, 1), -1, I32), jnp.zeros((n_rows, 1), I32)], axis=1)
    rows = rows.at[pos].set(jnp.stack([tok, gate_bits], axis=1))
    row_tok = rows[:, 0]
    row_gate = lax.bitcast_convert_type(rows[:, 1], F32)
    block_start = jnp.arange(n_blocks, dtype=I32) * EXPERT_ROWS
    block_expert = jnp.minimum(jnp.sum((pad_ends[None, :] <= block_start[:, None]).astype(I32), axis=1),
                               N_EXPERTS - 1).astype(I32)

    chunk = min(MOE_CHUNK, n)
    assert n % chunk == 0
    n_chunks = n // chunk
    rt2 = row_tok.reshape(n_blocks, EXPERT_ROWS)
    hi = jnp.max(rt2, axis=1)
    lo = jnp.min(jnp.where(rt2 >= 0, rt2, n), axis=1)
    lo_c = jnp.where(hi < 0, 0, lo // chunk)
    hi_c = jnp.where(hi < 0, 0, hi // chunk)
    n_ch = hi_c - lo_c + 1
    ends = jnp.cumsum(n_ch)
    starts = ends - n_ch
    total = ends[-1]
    n_pairs = N_EXPERTS * n_chunks + n_blocks
    p = jnp.arange(n_pairs, dtype=I32)
    valid = p < total
    pb = jnp.minimum(jnp.sum((ends[None, :] <= p[:, None]).astype(I32), axis=1), n_blocks - 1).astype(I32)
    pc = jnp.where(valid, lo_c[pb] + p - starts[pb], hi_c[n_blocks - 1]).astype(I32)
    flags = (jnp.where(valid & (p == starts[pb]), FLAG_FIRST, 0) + jnp.where(valid & (p == ends[pb] - 1), FLAG_LAST, 0)
             + jnp.where(valid, FLAG_VALID, 0)).astype(I32)
    y_rows = _moe_ffn(pb, pc, flags, block_expert, h16, row_tok, row_gate, wgu16, wdn16, chunk)

    order = jnp.argsort(jnp.where(valid, pc * n_blocks + pb, jnp.int32(2 ** 30)))
    pb2, pc2 = pb[order], pc[order]
    last_valid = jnp.maximum(total - 1, 0)
    pb2 = jnp.where(valid, pb2, pb2[last_valid])
    pc2 = jnp.where(valid, pc2, pc2[last_valid])
    prev_c = jnp.concatenate([jnp.full((1,), -1, I32), pc2[:-1]])
    next_c = jnp.concatenate([pc2[1:], jnp.full((1,), -1, I32)])
    flags2 = (jnp.where(valid & (pc2 != prev_c), FLAG_FIRST, 0)
              + jnp.where(valid & ((pc2 != next_c) | (p == total - 1)), FLAG_LAST, 0)
              + jnp.where(valid, FLAG_VALID, 0)).astype(I32)
    return _moe_combine(pb2, pc2, flags2, y_rows, row_tok, x2, g_final, chunk)


def kernel(x_prompt, x_sample, cache_kv_a, cache_k_idx, cache_k_b, cache_v_b, cache_mem_kv, page_table, mem_prompt, rel_bias, g_mix, w_in, g_kidx, lambda_q1, lambda_k1, lambda_q2, lambda_k2, g_subln, w_out, g_mem, g_memin, w_mem_q, w_mem_kv, w_mem_o, g_ffn, w_router1, b_router1, w_router2, b_router2, w_gate_up, w_down, g_final):
    depth = g_mix.shape[0]
    assert depth == 1
    layer = 0
    batch, seq, d = x_prompt.shape
    dec_batch, dec_seq, _ = x_sample.shape
    n_pages = page_table.shape[1]
    n_phys = cache_kv_a.shape[1]
    past = n_pages * LANES
    n_mem = mem_prompt.shape[1]
    mem_w = N_HEADS_MEM * HEAD_DIM
    assert cache_kv_a.shape[2] == LANES and dec_seq <= SAMPLE_ROWS

    table_a = rel_bias[:, :N_HEADS_A]
    table_b = rel_bias[:, N_HEADS_A:]
    lam_init = 0.8 - 0.6 * math.exp(-0.3 * layer)
    lam = (jnp.exp(jnp.sum(lambda_q1[layer] * lambda_k1[layer]).astype(F32))
           - jnp.exp(jnp.sum(lambda_q2[layer] * lambda_k2[layer]).astype(F32)) + lam_init).reshape(1)

    w = w_in[layer]
    offs = [0, 512, 1024, 1536, 2048, 2112, 2120, 2632, 3144, 3656]
    q_a, k_a, v_a, q_i, k_i, w_i, q_b, k_b, v_b = [w[:, offs[s]:offs[s + 1]] for s in range(9)]
    w_packed = jnp.concatenate(
        [q_a, k_a, v_a, q_i, k_i, k_i, jnp.pad(w_i, ((0, 0), (0, LANES - N_IDX_HEADS))), q_b, k_b, v_b],
        axis=1).astype(BF16)
    gk2 = jnp.concatenate([g_kidx[layer], g_kidx[layer]]).reshape(1, LANES)
    g_mix_l = g_mix[layer].reshape(1, d)
    w_out16 = w_out[layer].astype(BF16)
    wq16 = w_mem_q[layer].astype(BF16)
    wo16 = w_mem_o[layer].astype(BF16)
    wr = jnp.concatenate([w_router1[layer],
                          jnp.transpose(w_router2[layer], (1, 0, 2)).reshape(d, N_EXPERTS),
                          jnp.zeros((d, LANES - N_GROUPS - N_EXPERTS), F32)], axis=1)
    br = jnp.concatenate([b_router1[layer], b_router2[layer].reshape(N_EXPERTS),
                          jnp.zeros((LANES - N_GROUPS - N_EXPERTS,), F32)]).reshape(1, LANES)
    wgu16 = w_gate_up[layer].astype(BF16)
    wdn16 = w_down[layer].astype(BF16)
    g_sub = g_subln[layer].reshape(1, LANES)
    g_mem_l = g_mem[layer].reshape(1, d)
    g_ffn_l = g_ffn[layer].reshape(1, d)
    g_fin = g_final.reshape(1, d)

    n_p = batch * seq
    (qa, kva, ka16, vat, qi, ki, ki16, wi, qb, kb, vb, kb16, vbt) = _proj(
        x_prompt.reshape(n_p, d), g_mix_l, w_packed, gk2)
    t_attn = min(ATTN_TILE, seq)
    o_a = _dsa_prompt(qa, qi, wi, ki16, ka16, vat, _prompt_bias_tiles(table_a, t_attn), batch, seq)
    o_b = _diff_prompt(lam, lam_init, qb, kb16, vbt, _prompt_bias_tiles(table_b, t_attn), g_sub, batch, seq)
    mkv_p = _norm_matmul(mem_prompt.reshape(batch * n_mem, d), g_memin[layer].reshape(1, d),
                         w_mem_kv[layer].astype(BF16))
    x1, qm = _tail1(x_prompt.reshape(n_p, d), [(o_a, w_out16[:A_WIDTH]), (o_b, w_out16[A_WIDTH:])],
                    g_mem_l, wq16)
    x2, h_ffn, route = _tail2(qm.reshape(batch, seq, mem_w), mkv_p.reshape(batch, n_mem, 2 * mem_w),
                              x1.reshape(batch, seq, d), wo16, g_ffn_l, wr, br)
    y_prompt = _moe_and_final(x2.reshape(n_p, d), h_ffn.reshape(n_p, d), route.reshape(n_p, LANES),
                              wgu16, wdn16, g_fin).reshape(batch, seq, d)

    n_s = dec_batch * dec_seq
    (qa_s, kva_s, _, _, qi_s, ki_s, ki16_s, wi_s, qb_s, kb_s, vb_s, kb16_s, _) = _proj(
        x_sample.reshape(n_s, d), g_mix_l, w_packed, gk2)
    kva16_s = kva_s.astype(BF16)
    pad_t = SAMPLE_ROWS - dec_seq
    qx = jnp.transpose(jnp.pad(qi_s.reshape(dec_batch, dec_seq, N_IDX_HEADS, D_IDX),
                               ((0, 0), (0, pad_t), (0, 0), (0, 0))), (0, 2, 1, 3)
                       ).reshape(dec_batch, N_IDX_HEADS * SAMPLE_ROWS, D_IDX)
    wx = jnp.transpose(jnp.pad(wi_s[:, :N_IDX_HEADS].reshape(dec_batch, dec_seq, N_IDX_HEADS),
                               ((0, 0), (0, pad_t), (0, 0))), (0, 2, 1)
                       ).reshape(dec_batch, N_IDX_HEADS * SAMPLE_ROWS, 1)
    wx = jnp.broadcast_to(wx, (dec_batch, N_IDX_HEADS * SAMPLE_ROWS, LANES))

    def new_chunk_t(a):
        a = jnp.pad(a.reshape(dec_batch, dec_seq, a.shape[1]), ((0, 0), (0, LANES - dec_seq), (0, 0)))
        return jnp.transpose(a, (0, 2, 1))

    kidx_t = jnp.transpose(cache_k_idx[layer], (0, 2, 1))
    kva_t = jnp.transpose(cache_kv_a[layer], (0, 2, 3, 4, 1)).reshape(n_phys, 2 * A_WIDTH, LANES)
    kb_t = jnp.transpose(cache_k_b[layer], (0, 2, 3, 4, 1)).reshape(n_phys, B_WIDTH, LANES)
    vb_r = cache_v_b[layer].reshape(n_phys, LANES * N_HEADS_B, 2 * HEAD_DIM)
    vb_new = jnp.pad(vb_s.reshape(dec_batch, dec_seq, N_HEADS_B, 2 * HEAD_DIM),
                     ((0, 0), (0, LANES - dec_seq), (0, 0), (0, 0))).reshape(dec_batch, LANES * N_HEADS_B, 2 * HEAD_DIM)

    g_idx = 32 if n_pages % 32 == 0 else 1
    keys = _sample_scores(page_table, kidx_t, qx, wx, new_chunk_t(ki16_s[:, :D_IDX]), dec_seq, g_idx)
    mask = _sample_select(keys, min(TOPK_MAX, (past + dec_seq) // 4))
    g_att = 16 if n_pages % 16 == 0 else 1
    o_s = _sample_attn(
        lam, lam_init, page_table, kva_t, kb_t, vb_r,
        _block_diag_rows(qa_s.reshape(dec_batch, dec_seq, N_HEADS_A, HEAD_DIM)),
        _block_diag_rows(qb_s.reshape(dec_batch, dec_seq, 2 * N_HEADS_B, HEAD_DIM)),
        mask, _sample_bias(table_a, past, dec_seq, False), _sample_bias(table_b, past, dec_seq, True),
        new_chunk_t(kva16_s), new_chunk_t(kb16_s), vb_new, g_sub, g_att)
    o_s = o_s[:, :dec_seq].reshape(n_s, A_WIDTH + B_WIDTH)
    x1_s, qm_s = _tail1(x_sample.reshape(n_s, d), [(o_s, w_out16)], g_mem_l, wq16)
    x2_s, h_s, route_s = _tail2(qm_s.reshape(dec_batch, dec_seq, mem_w),
                                cache_mem_kv[layer].reshape(dec_batch, n_mem, 2 * mem_w),
                                x1_s.reshape(dec_batch, dec_seq, d), wo16, g_ffn_l, wr, br)
    y_sample = _moe_and_final(x2_s.reshape(n_s, d), h_s.reshape(n_s, d), route_s.reshape(n_s, LANES),
                              wgu16, wdn16, g_fin).reshape(dec_batch, dec_seq, d)

    return (y_prompt, y_sample,
            kva.reshape(1, batch, seq, 2, N_HEADS_A, HEAD_DIM),
            ki[:, :D_IDX].reshape(1, batch, seq, D_IDX),
            kb.reshape(1, batch, seq, N_HEADS_B, 2, HEAD_DIM),
            vb.reshape(1, batch, seq, N_HEADS_B, 2 * HEAD_DIM),
            mkv_p.reshape(1, batch, n_mem, 2, N_HEADS_MEM, HEAD_DIM),
            kva_s.reshape(1, dec_batch, dec_seq, 2, N_HEADS_A, HEAD_DIM),
            ki_s[:, :D_IDX].reshape(1, dec_batch, dec_seq, D_IDX),
            kb_s.reshape(1, dec_batch, dec_seq, N_HEADS_B, 2, HEAD_DIM),
            vb_s.reshape(1, dec_batch, dec_seq, N_HEADS_B, 2 * HEAD_DIM))
```

```python
import functools
import math

import jax
import jax.numpy as jnp
from jax import lax
from jax.experimental import pallas as pl
from jax.experimental.pallas import tpu as pltpu

F32 = jnp.float32
BF16 = jnp.bfloat16
I32 = jnp.int32

HEAD_DIM = 64
N_HEADS_A = 8
N_HEADS_B = 4
A_WIDTH = 512
B_WIDTH = 512
N_IDX_HEADS = 8
D_IDX = 64
TOPK_MAX = 256
N_HEADS_MEM = 4
N_BUCKETS = 32
MAX_EXACT = 16
MAX_DISTANCE = 128
N_GROUPS = 4
EXPERTS_PER_GROUP = 8
N_EXPERTS = 32
D_EXPERT = 512
NEG = -1e30
EPS = 1e-6
LOG2E = math.log2(math.e)
LANES = 128
INT_MIN = -(2 ** 31)

SEG_QA, SEG_KVA, SEG_QI, SEG_KI, SEG_WI, SEG_QB, SEG_KB, SEG_VB = (
    (0, 512), (512, 1536), (1536, 2048), (2048, 2176), (2176, 2304), (2304, 2816), (2816, 3328), (3328, 3840))
PACKED_COLS = 3840

VMEM_LIMIT = 56 * 1024 * 1024
ATTN_TILE = 256
KEY_SUB = 128
EXPERT_ROWS = 256
MOE_CHUNK = 1024


def _cparams(sem):
    return pltpu.CompilerParams(dimension_semantics=sem, vmem_limit_bytes=VMEM_LIMIT)


def _dot(a, b):
    return jnp.dot(a, b, preferred_element_type=F32)


def _dot_nt(a, b):
    return lax.dot_general(a, b, (((1,), (1,)), ((), ())), preferred_element_type=F32)


def _rms(x, g):
    return x * lax.rsqrt(jnp.mean(x * x, axis=-1, keepdims=True) + EPS) * g


def _sortable_key(score):
    bits = lax.bitcast_convert_type(score, I32)
    key = jnp.where(bits < 0, bits ^ jnp.int32(0x7FFFFFFF), bits)
    return jnp.where(bits == jnp.int32(INT_MIN), 0, key)


def _proj_kernel(q_scale, x_ref, g_ref, w_ref, gk_ref, qa_ref, kva_ref, ka16_ref, vat_ref, qi_ref, ki_ref, ki16_ref,
                 wi_ref, qb_ref, kb_ref, vb_ref, kb16_ref, vbt_ref):
    h = _rms(x_ref[...], g_ref[...]).astype(BF16)
    tt = vat_ref.shape[2]

    def seg(s):
        return _dot(h, w_ref[:, s[0]:s[1]])

    def store_transposed(dst_ref, v):
        for c in range(dst_ref.shape[0]):
            dst_ref[c] = v[c * tt:(c + 1) * tt, :].T.astype(BF16)

    qa_ref[...] = (seg(SEG_QA) * (HEAD_DIM ** -0.5 * q_scale)).astype(BF16)
    kva = seg(SEG_KVA)
    kva_ref[...] = kva
    ka16_ref[...] = kva[:, :A_WIDTH].astype(BF16)
    store_transposed(vat_ref, kva[:, A_WIDTH:])
    qi_ref[...] = (seg(SEG_QI) * D_IDX ** -0.5).astype(BF16)
    ki = _rms(seg(SEG_KI), gk_ref[...])
    ki_ref[...] = ki
    ki16_ref[...] = ki.astype(BF16)
    wi_ref[...] = seg(SEG_WI) * N_IDX_HEADS ** -0.5
    qb_ref[...] = (seg(SEG_QB) * (HEAD_DIM ** -0.5 * q_scale)).astype(BF16)
    kb = seg(SEG_KB)
    kb_ref[...] = kb
    kb16_ref[...] = kb.astype(BF16)
    vb = seg(SEG_VB)
    vb_ref[...] = vb
    store_transposed(vbt_ref, vb)


def _proj(x, g, w_packed, gk2, q_scale=1.0):
    n, d = x.shape
    tm = min(n, 512)
    tt = min(ATTN_TILE, tm)
    vt = (A_WIDTH, None)
    widths = [(512, BF16), (1024, F32), (512, BF16), vt, (512, BF16), (128, F32), (128, BF16), (128, F32),
              (512, BF16), (512, F32), (512, F32), (512, BF16), vt]

    def spec(w, dt):
        if dt is None:
            return pl.BlockSpec((tm // tt, w, tt), lambda i: (i, 0, 0))
        return pl.BlockSpec((tm, w), lambda i: (i, 0))

    def shape(w, dt):
        if dt is None:
            return jax.ShapeDtypeStruct((n // tt, w, tt), BF16)
        return jax.ShapeDtypeStruct((n, w), dt)

    return pl.pallas_call(
        functools.partial(_proj_kernel, q_scale),
        grid=(n // tm,),
        in_specs=[pl.BlockSpec((tm, d), lambda i: (i, 0)),
                  pl.BlockSpec((1, d), lambda i: (0, 0)),
                  pl.BlockSpec((d, PACKED_COLS), lambda i: (0, 0)),
                  pl.BlockSpec((1, LANES), lambda i: (0, 0))],
        out_specs=[spec(w, dt) for w, dt in widths],
        out_shape=[shape(w, dt) for w, dt in widths],
        compiler_params=_cparams(("arbitrary",)),
        name="proj",
    )(x, g, w_packed, gk2)


def _norm_matmul_kernel(x_ref, g_ref, w_ref, o_ref):
    o_ref[...] = _dot(_rms(x_ref[...], g_ref[...]).astype(BF16), w_ref[...])


def _norm_matmul(x, g, w16):
    n, d = x.shape
    tm = min(n, 512)
    return pl.pallas_call(
        _norm_matmul_kernel,
        grid=(n // tm,),
        in_specs=[pl.BlockSpec((tm, d), lambda i: (i, 0)),
                  pl.BlockSpec((1, d), lambda i: (0, 0)),
                  pl.BlockSpec(w16.shape, lambda i: (0, 0))],
        out_specs=pl.BlockSpec((tm, w16.shape[1]), lambda i: (i, 0)),
        out_shape=jax.ShapeDtypeStruct((n, w16.shape[1]), F32),
        compiler_params=_cparams(("arbitrary",)),
        name="norm_matmul",
    )(x, g, w16)


def _split_halves(q_ref, dst_ref, n_pairs):
    t = q_ref.shape[0]
    lo = lax.broadcasted_iota(I32, (t, LANES), 1) < HEAD_DIM
    for p in range(n_pairs):
        qp = q_ref[:, p * LANES:(p + 1) * LANES]
        dst_ref[2 * p] = jnp.where(lo, qp, jnp.zeros_like(qp))
        dst_ref[2 * p + 1] = jnp.where(lo, jnp.zeros_like(qp), qp)


def _online_softmax_step(s, h, m_ref, l_ref):
    m_prev = m_ref[h]
    m_new = jnp.maximum(m_prev, jnp.max(s, axis=1, keepdims=True))
    alpha = jnp.exp(m_prev - m_new)
    p = jnp.exp(s - jnp.concatenate([m_new] * (s.shape[1] // LANES), axis=1))
    l_ref[h] = alpha * l_ref[h] + jnp.sum(p, axis=1, keepdims=True)
    m_ref[h] = m_new
    return p, alpha


def _online_softmax_step_t(s, h, m_ref, l_ref):
    m_prev = m_ref[h]
    m_new = jnp.maximum(m_prev, jnp.max(s, axis=0, keepdims=True))
    alpha = jnp.exp2(m_prev - m_new)
    p = jnp.exp2(s - m_new)
    l_ref[h] = alpha * l_ref[h] + jnp.sum(p, axis=0, keepdims=True)
    m_ref[h] = m_new
    return p, alpha


def _t5_bucket(dist):
    n = jnp.maximum(dist, 0)
    nf = jnp.maximum(n, 1).astype(F32)
    large = MAX_EXACT + (jnp.log(nf / MAX_EXACT) / math.log(MAX_DISTANCE / MAX_EXACT)
                         * (N_BUCKETS - MAX_EXACT)).astype(I32)
    large = jnp.minimum(large, N_BUCKETS - 1)
    return jnp.where(n < MAX_EXACT, n, large)


def _bias_lookup(table, dist):
    onehot = jax.nn.one_hot(_t5_bucket(dist), N_BUCKETS, dtype=F32)
    return jnp.dot(onehot, table.astype(F32), precision=lax.Precision.HIGHEST)


def _prompt_bias_tiles(table, t):
    assert t >= MAX_DISTANCE
    r = jnp.arange(t, dtype=I32)[None, :]
    c = jnp.arange(t, dtype=I32)[:, None]
    tiles = []
    for delta in range(3):
        dist = delta * t + r - c
        b = jnp.transpose(_bias_lookup(table, dist), (2, 0, 1)) * LOG2E
        tiles.append(jnp.where(dist[None] >= 0, b, NEG))
    return jnp.stack(tiles)


def _dsa_prompt_kernel(k_sel, n_bits, qa_ref, qi_ref, wi_ref, k2_ref, k_ref, vt_ref, bt_ref, o_ref,
                       qah_ref, qih_ref, key_ref, m_ref, l_ref, acc_ref):
    t = qa_ref.shape[0]
    i = pl.program_id(1)
    nj = i + 1
    kidx = lax.broadcasted_iota(I32, (t, t), 0)
    qidx = lax.broadcasted_iota(I32, (t, t), 1)

    _split_halves(qa_ref, qah_ref, N_HEADS_A // 2)
    _split_halves(qi_ref, qih_ref, N_IDX_HEADS // 2)
    wi_t = wi_ref[...].T

    def score_chunk(j, carry):
        k0 = pl.multiple_of(j * t, t)
        for ks in range(0, t, KEY_SUB):
            keys = slice(ks, ks + KEY_SUB)
            kc = k2_ref[pl.ds(k0 + ks, KEY_SUB), :]
            acc = jnp.zeros((KEY_SUB, t), F32)
            for h in range(N_IDX_HEADS):
                acc = acc + jnp.maximum(_dot_nt(kc, qih_ref[h]), 0.0) * wi_t[h:h + 1, :]
            key = _sortable_key(acc)
            kpos = k0 + ks + lax.broadcasted_iota(I32, (KEY_SUB, t), 0)
            qpos = i * t + lax.broadcasted_iota(I32, (KEY_SUB, t), 1)
            key_ref[j, keys, :] = jnp.where(kpos <= qpos, key, jnp.int32(INT_MIN))
        return carry

    lax.fori_loop(0, nj, score_chunk, 0)

    def count_where(pred_fn):
        def body(j, c):
            hit = jnp.where(pred_fn(key_ref[j], j), 1.0, 0.0)
            return c + jnp.sum(hit.reshape(t // 8, 8, t), axis=0)
        return jnp.sum(lax.fori_loop(0, nj, body, jnp.zeros((8, t), F32)), axis=0, keepdims=True)

    k_f = float(k_sel)
    n_valid = (i * t + lax.broadcasted_iota(I32, (1, t), 1) + 1).astype(F32)

    def search_cond(state):
        b, _, cnt = state
        return (b < 32) & (jnp.max(cnt) > k_f)

    def search_step(state):
        b, tu, cnt = state
        cand_u = tu | lax.shift_left(jnp.int32(1), 31 - b)
        cand = cand_u ^ jnp.int32(INT_MIN)
        c = count_where(lambda kc, j: kc >= cand)
        take = c >= k_f
        return b + 1, jnp.where(take, cand_u, tu), jnp.where(take, c, cnt)

    _, tu, cnt = lax.while_loop(search_cond, lambda state: search_step(search_step(state)),
                                (jnp.int32(0), jnp.zeros((1, t), I32), n_valid))
    thr = tu ^ jnp.int32(INT_MIN)

    def tie_search(_):
        need = k_f - count_where(lambda kc, j: kc > thr)

        def cut_step(b, c0):
            cand = c0 | lax.shift_left(jnp.int32(1), n_bits - 1 - b)
            c = count_where(lambda kc, j: (kc == thr) & (j * t + kidx < cand))
            return jnp.where(c < need, cand, c0)
        return lax.fori_loop(0, n_bits, cut_step, jnp.zeros((1, t), I32))

    cut = lax.cond(jnp.max(cnt) > k_f, tie_search, lambda _: jnp.full((1, t), 2 ** 30, I32), 0)

    m_ref[...] = jnp.full(m_ref.shape, NEG, F32)
    l_ref[...] = jnp.zeros(l_ref.shape, F32)
    acc_ref[...] = jnp.zeros(acc_ref.shape, F32)

    def attn_chunk(j, carry):
        k0 = pl.multiple_of(j * t, t)
        kc = key_ref[j]
        sel = (kc > thr) | ((kc == thr) & (k0 + kidx <= cut))
        dsel = jnp.minimum(i - j, 2)
        for ks in range(0, t, KEY_SUB):
            keys = slice(ks, ks + KEY_SUB)
            for p in range(N_HEADS_A // 2):
                kp = k_ref[pl.ds(k0 + ks, KEY_SUB), p * LANES:(p + 1) * LANES]
                vp_t = vt_ref[j, p * LANES:(p + 1) * LANES, keys]
                for half in range(2):
                    h = 2 * p + half
                    s = jnp.where(sel[keys], _dot_nt(kp, qah_ref[h]) + bt_ref[dsel, h, keys, :], NEG)
                    pr, alpha = _online_softmax_step_t(s, h, m_ref, l_ref)
                    pv = _dot(vp_t, pr.astype(BF16))
                    rows = slice(half * HEAD_DIM, (half + 1) * HEAD_DIM)
                    acc_ref[p, rows, :] = acc_ref[p, rows, :] * alpha + pv[rows]
        return carry

    lax.fori_loop(0, nj, attn_chunk, 0)

    for p in range(N_HEADS_A // 2):
        o_t = jnp.concatenate([acc_ref[p, :HEAD_DIM, :] / l_ref[2 * p],
                               acc_ref[p, HEAD_DIM:, :] / l_ref[2 * p + 1]], axis=0)
        o_ref[:, p * LANES:(p + 1) * LANES] = o_t.T.astype(BF16)


def _dsa_prompt(qa, qi, wi, ki16, ka16, vat, bias_tiles, batch, seq):
    t = min(ATTN_TILE, seq)
    nq = seq // t
    k_sel = min(TOPK_MAX, seq // 4)
    n_bits = max(1, (seq - 1).bit_length())
    kern = functools.partial(_dsa_prompt_kernel, k_sel, n_bits)
    return pl.pallas_call(
        kern,
        grid=(batch, nq),
        in_specs=[pl.BlockSpec((t, A_WIDTH), lambda b, i: (b * nq + i, 0)),
                  pl.BlockSpec((t, A_WIDTH), lambda b, i: (b * nq + i, 0)),
                  pl.BlockSpec((t, LANES), lambda b, i: (b * nq + i, 0)),
                  pl.BlockSpec((seq, LANES), lambda b, i: (b, 0)),
                  pl.BlockSpec((seq, A_WIDTH), lambda b, i: (b, 0)),
                  pl.BlockSpec((nq, A_WIDTH, t), lambda b, i: (b, 0, 0)),
                  pl.BlockSpec(bias_tiles.shape, lambda b, i: (0, 0, 0, 0))],
        out_specs=pl.BlockSpec((t, A_WIDTH), lambda b, i: (b * nq + i, 0)),
        out_shape=jax.ShapeDtypeStruct((batch * seq, A_WIDTH), BF16),
        scratch_shapes=[pltpu.VMEM((N_HEADS_A, t, LANES), BF16),
                        pltpu.VMEM((N_IDX_HEADS, t, LANES), BF16),
                        pltpu.VMEM((nq, t, t), I32),
                        pltpu.VMEM((N_HEADS_A, 1, t), F32),
                        pltpu.VMEM((N_HEADS_A, 1, t), F32),
                        pltpu.VMEM((N_HEADS_A // 2, LANES, t), F32)],
        compiler_params=_cparams(("arbitrary", "arbitrary")),
        name="dsa_prompt",
    )(qa, qi, wi, ki16, ka16, vat, bias_tiles)


def _subln(o0, o1, lam, lam_init, g):
    d = o0 - lam * o1
    return _rms(d, g) * (1.0 - lam_init)


def _diff_prompt_kernel(lam_init, lam_ref, qb_ref, kb_ref, vbt_ref, bt_ref, g_ref, o_ref,
                        qh_ref, m_ref, l_ref, acc_ref):
    t = qb_ref.shape[0]
    i = pl.program_id(1)
    _split_halves(qb_ref, qh_ref, N_HEADS_B)
    m_ref[...] = jnp.full(m_ref.shape, NEG, F32)
    l_ref[...] = jnp.zeros(l_ref.shape, F32)
    acc_ref[...] = jnp.zeros(acc_ref.shape, F32)

    def chunk(j, carry):
        k0 = pl.multiple_of(j * t, t)
        dsel = jnp.minimum(i - j, 2)
        for ks in range(0, t, KEY_SUB):
            keys = slice(ks, ks + KEY_SUB)
            for h in range(N_HEADS_B):
                kp = kb_ref[pl.ds(k0 + ks, KEY_SUB), h * LANES:(h + 1) * LANES]
                vp_t = vbt_ref[j, h * LANES:(h + 1) * LANES, keys]
                for mp in range(2):
                    c = 2 * h + mp
                    s = _dot_nt(kp, qh_ref[c]) + bt_ref[dsel, c, keys, :]
                    pr, alpha = _online_softmax_step_t(s, c, m_ref, l_ref)
                    acc_ref[c] = acc_ref[c] * alpha + _dot(vp_t, pr.astype(BF16))
        return carry

    lax.fori_loop(0, i + 1, chunk, 0)

    lam = lam_ref[0]
    for h in range(N_HEADS_B):
        o0 = (acc_ref[2 * h] / l_ref[2 * h]).T
        o1 = (acc_ref[2 * h + 1] / l_ref[2 * h + 1]).T
        o_ref[:, h * LANES:(h + 1) * LANES] = _subln(o0, o1, lam, lam_init, g_ref[...]).astype(BF16)


def _diff_prompt(lam, lam_init, qb, kb16, vbt, bias_tiles, g_sub, batch, seq):
    t = min(ATTN_TILE, seq)
    nq = seq // t
    n_maps = 2 * N_HEADS_B
    kern = functools.partial(_diff_prompt_kernel, lam_init)
    return pl.pallas_call(
        kern,
        grid=(batch, nq),
        in_specs=[pl.BlockSpec(memory_space=pltpu.SMEM),
                  pl.BlockSpec((t, B_WIDTH), lambda b, i: (b * nq + i, 0)),
                  pl.BlockSpec((seq, B_WIDTH), lambda b, i: (b, 0)),
                  pl.BlockSpec((nq, B_WIDTH, t), lambda b, i: (b, 0, 0)),
                  pl.BlockSpec(bias_tiles.shape, lambda b, i: (0, 0, 0, 0)),
                  pl.BlockSpec((1, LANES), lambda b, i: (0, 0))],
        out_specs=pl.BlockSpec((t, B_WIDTH), lambda b, i: (b * nq + i, 0)),
        out_shape=jax.ShapeDtypeStruct((batch * seq, B_WIDTH), BF16),
        scratch_shapes=[pltpu.VMEM((n_maps, t, LANES), BF16),
                        pltpu.VMEM((n_maps, 1, t), F32),
                        pltpu.VMEM((n_maps, 1, t), F32),
                        pltpu.VMEM((n_maps, LANES, t), F32)],
        compiler_params=_cparams(("arbitrary", "arbitrary")),
        name="diff_prompt",
    )(lam, qb, kb16, vbt, bias_tiles, g_sub)


SAMPLE_ROWS = 8


def _sample_scores_kernel(n_pages, g_pages, dec_seq, pt_ref, *refs):
    page_refs = refs[:g_pages]
    qx_ref, wx_ref, knew_ref, key_ref = refs[g_pages:]
    del pt_ref
    step = pl.program_id(1)
    n_steps = n_pages // g_pages
    r8 = SAMPLE_ROWS

    def chunk_scores(kt16):
        d = jnp.maximum(_dot(qx_ref[0], kt16), 0.0) * wx_ref[0]
        acc = d[0:r8]
        for h in range(1, N_IDX_HEADS):
            acc = acc + d[h * r8:(h + 1) * r8]
        return _sortable_key(acc)

    for g in range(g_pages):
        key_ref[0, step * g_pages + g] = chunk_scores(page_refs[g][0].astype(BF16))

    @pl.when(step == n_steps - 1)
    def _():
        row = lax.broadcasted_iota(I32, (r8, LANES), 0)
        lane = lax.broadcasted_iota(I32, (r8, LANES), 1)
        key_new = chunk_scores(knew_ref[0])
        key_ref[0, n_pages] = jnp.where((lane <= row) & (lane < dec_seq), key_new, jnp.int32(INT_MIN))


def _sample_scores(page_table, cache_kidx_t, qx, wx, knew_t, dec_seq, g_pages):
    batch, n_pages = page_table.shape
    n_steps = n_pages // g_pages
    rows = N_IDX_HEADS * SAMPLE_ROWS
    kern = functools.partial(_sample_scores_kernel, n_pages, g_pages, dec_seq)
    page_specs = [pl.BlockSpec((1, D_IDX, LANES), functools.partial(
        lambda b, s, pt, g: (pt[b, s * g_pages + g], 0, 0), g=g)) for g in range(g_pages)]
    grid_spec = pltpu.PrefetchScalarGridSpec(
        num_scalar_prefetch=1,
        grid=(batch, n_steps),
        in_specs=page_specs + [pl.BlockSpec((1, rows, D_IDX), lambda b, s, pt: (b, 0, 0)),
                               pl.BlockSpec((1, rows, LANES), lambda b, s, pt: (b, 0, 0)),
                               pl.BlockSpec((1, D_IDX, LANES), lambda b, s, pt: (b, 0, 0))],
        out_specs=pl.BlockSpec((1, n_pages + 1, SAMPLE_ROWS, LANES), lambda b, s, pt: (b, 0, 0, 0)))
    return pl.pallas_call(
        kern,
        grid_spec=grid_spec,
        out_shape=jax.ShapeDtypeStruct((batch, n_pages + 1, SAMPLE_ROWS, LANES), I32),
        compiler_params=_cparams(("arbitrary", "arbitrary")),
        name="sample_scores",
    )(page_table, *([cache_kidx_t] * g_pages), qx, wx, knew_t)


def _sample_select_kernel(k_sel, n_bits, key_ref, mask_ref):
    shape = key_ref.shape
    colg = lax.broadcasted_iota(I32, shape, 1) * LANES + lax.broadcasted_iota(I32, shape, 3)

    def count(pred):
        c = jnp.sum(jnp.where(pred, 1.0, 0.0), axis=1, keepdims=True)
        return jnp.sum(c, axis=3, keepdims=True)

    stat = (shape[0], 1, shape[2], 1)

    def bit_step(b, tu):
        cand_u = tu | lax.shift_left(jnp.int32(1), 31 - b)
        cand = cand_u ^ jnp.int32(INT_MIN)
        return jnp.where(count(key_ref[...] >= cand) >= k_sel, cand_u, tu)

    tu = lax.fori_loop(0, 32, bit_step, jnp.zeros(stat, I32))
    thr = tu ^ jnp.int32(INT_MIN)
    need = k_sel - count(key_ref[...] > thr)

    def cut_step(b, c0):
        cand = c0 | lax.shift_left(jnp.int32(1), n_bits - 1 - b)
        return jnp.where(count((key_ref[...] == thr) & (colg < cand)) < need, cand, c0)

    cut = lax.fori_loop(0, n_bits, cut_step, jnp.zeros(stat, I32))
    keys = key_ref[...]
    sel = (keys > thr) | ((keys == thr) & (colg <= cut))
    sel = sel & (keys != jnp.int32(INT_MIN))
    mask_ref[...] = jnp.where(sel, 0.0, NEG)


def _sample_select(keys, k_sel):
    batch, n_chunks = keys.shape[:2]
    bt = 8 if batch % 8 == 0 else 1
    n_bits = max(1, (n_chunks * LANES - 1).bit_length())
    blk = (bt, n_chunks, SAMPLE_ROWS, LANES)
    return pl.pallas_call(
        functools.partial(_sample_select_kernel, k_sel, n_bits),
        grid=(batch // bt,),
        in_specs=[pl.BlockSpec(blk, lambda i: (i, 0, 0, 0))],
        out_specs=pl.BlockSpec(blk, lambda i: (i, 0, 0, 0)),
        out_shape=jax.ShapeDtypeStruct(keys.shape, F32),
        compiler_params=_cparams(("arbitrary",)),
        name="sample_select",
    )(keys)


def _sample_attn_kernel(lam_init, n_pages, g_pages, pt_ref, lam_ref, *refs):
    kva_refs = refs[:g_pages]
    kb_refs = refs[g_pages:2 * g_pages]
    vb_refs = refs[2 * g_pages:3 * g_pages]
    (qa_ref, qb_ref, mask_ref, maskn_ref, ba_ref, bb_ref, kvan_ref, kbn_ref, vbn_ref, g_ref,
     o_ref, m_ref, l_ref, acca_ref, accb_ref) = refs[3 * g_pages:]
    del pt_ref
    step = pl.program_id(1)
    n_steps = n_pages // g_pages
    r8 = SAMPLE_ROWS

    @pl.when(step == 0)
    def _():
        m_ref[...] = jnp.full(m_ref.shape, NEG, F32)
        l_ref[...] = jnp.zeros(l_ref.shape, F32)
        acca_ref[...] = jnp.zeros(acca_ref.shape, F32)
        accb_ref[...] = jnp.zeros(accb_ref.shape, F32)

    def both(kva, kb, vb, mask8, bsel):
        n = len(kva)
        cols = [slice(g * LANES, (g + 1) * LANES) for g in range(n)]
        s = jnp.concatenate(
            [_dot(qa_ref[0], kva[g][0, :A_WIDTH, :].astype(BF16))
             + (ba_ref[bsel[g]] + jnp.concatenate([mask8[g]] * N_HEADS_A, axis=0)) for g in range(n)], axis=1)
        pr, alpha = _online_softmax_step(s, 0, m_ref, l_ref)
        pr = pr.astype(BF16)
        pv = _dot_nt(pr[:, cols[0]], kva[0][0, A_WIDTH:, :].astype(BF16))
        for g in range(1, n):
            pv = pv + _dot_nt(pr[:, cols[g]], kva[g][0, A_WIDTH:, :].astype(BF16))
        acca_ref[...] = acca_ref[...] * jnp.concatenate([alpha] * (A_WIDTH // LANES), axis=1) + pv
        s = jnp.concatenate([_dot(qb_ref[0], kb[g][0].astype(BF16)) + bb_ref[bsel[g]] for g in range(n)], axis=1)
        pr, alpha = _online_softmax_step(s, 1, m_ref, l_ref)
        pr = pr.astype(BF16)
        pv = []
        for h in range(N_HEADS_B):
            rows = slice(2 * r8 * h, 2 * r8 * (h + 1))
            acc = None
            for g in range(n):
                d = _dot(pr[rows, cols[g]], vb[g][0, pl.ds(h, LANES, stride=N_HEADS_B), :].astype(BF16))
                acc = d if acc is None else acc + d
            pv.append(acc)
        accb_ref[...] = accb_ref[...] * alpha + jnp.concatenate(pv, axis=0)

    bsel = [0] * (g_pages - 1) + [jnp.where(step == n_steps - 1, 1, 0)]
    both(kva_refs, kb_refs, vb_refs, [mask_ref[0, g] for g in range(g_pages)], bsel)

    @pl.when(step == n_steps - 1)
    def _():
        both([kvan_ref], [kbn_ref], [vbn_ref], [maskn_ref[0, 0]], [2])
        lane = lax.broadcasted_iota(I32, (r8, A_WIDTH), 1)
        oa = jnp.zeros((r8, A_WIDTH), F32)
        la = jnp.concatenate([l_ref[0]] * (A_WIDTH // LANES), axis=1)
        acca = acca_ref[...] / la
        for h in range(N_HEADS_A):
            blk = acca[h * r8:(h + 1) * r8]
            oa = jnp.where((lane >= h * HEAD_DIM) & (lane < (h + 1) * HEAD_DIM), blk, oa)
        o_ref[0, :, :A_WIDTH] = oa.astype(BF16)
        lam = lam_ref[0]
        ob = accb_ref[...] / l_ref[1]
        for h in range(N_HEADS_B):
            r0 = 2 * h * r8
            o_ref[0, :, A_WIDTH + h * LANES:A_WIDTH + (h + 1) * LANES] = _subln(
                ob[r0:r0 + r8], ob[r0 + r8:r0 + 2 * r8], lam, lam_init, g_ref[...]).astype(BF16)


def _sample_attn(lam, lam_init, page_table, cache_kva_t, cache_kb_t, cache_vb, qa_x, qb_x, mask, bias_a, bias_b,
                 kva_new_t, kb_new_t, vb_new, g_sub, g_pages):
    batch, n_pages = page_table.shape
    n_steps = n_pages // g_pages
    rows = N_HEADS_A * SAMPLE_ROWS
    kern = functools.partial(_sample_attn_kernel, lam_init, n_pages, g_pages)

    def page_spec(width, g):
        return pl.BlockSpec((1, width, LANES), lambda b, s, pt: (pt[b, s * g_pages + g], 0, 0))

    in_specs = ([pl.BlockSpec(memory_space=pltpu.SMEM)]
                + [page_spec(2 * A_WIDTH, g) for g in range(g_pages)]
                + [page_spec(B_WIDTH, g) for g in range(g_pages)]
                + [page_spec(B_WIDTH, g) for g in range(g_pages)]
                + [pl.BlockSpec((1, rows, A_WIDTH), lambda b, s, pt: (b, 0, 0)),
                   pl.BlockSpec((1, rows, B_WIDTH), lambda b, s, pt: (b, 0, 0)),
                   pl.BlockSpec((1, g_pages, SAMPLE_ROWS, LANES), lambda b, s, pt: (b, s, 0, 0)),
                   pl.BlockSpec((1, 1, SAMPLE_ROWS, LANES), lambda b, s, pt: (b, n_pages, 0, 0)),
                   pl.BlockSpec(bias_a.shape, lambda b, s, pt: (0, 0, 0)),
                   pl.BlockSpec(bias_b.shape, lambda b, s, pt: (0, 0, 0)),
                   pl.BlockSpec((1, 2 * A_WIDTH, LANES), lambda b, s, pt: (b, 0, 0)),
                   pl.BlockSpec((1, B_WIDTH, LANES), lambda b, s, pt: (b, 0, 0)),
                   pl.BlockSpec((1, B_WIDTH, LANES), lambda b, s, pt: (b, 0, 0)),
                   pl.BlockSpec((1, LANES), lambda b, s, pt: (0, 0))])
    grid_spec = pltpu.PrefetchScalarGridSpec(
        num_scalar_prefetch=1,
        grid=(batch, n_steps),
        in_specs=in_specs,
        out_specs=pl.BlockSpec((1, SAMPLE_ROWS, A_WIDTH + B_WIDTH), lambda b, s, pt: (b, 0, 0)),
        scratch_shapes=[pltpu.VMEM((2, rows, LANES), F32),
                        pltpu.VMEM((2, rows, LANES), F32),
                        pltpu.VMEM((rows, A_WIDTH), F32),
                        pltpu.VMEM((rows, LANES), F32)])
    return pl.pallas_call(
        kern,
        grid_spec=grid_spec,
        out_shape=jax.ShapeDtypeStruct((batch, SAMPLE_ROWS, A_WIDTH + B_WIDTH), BF16),
        compiler_params=_cparams(("arbitrary", "arbitrary")),
        name="sample_attn",
    )(page_table, lam, *([cache_kva_t] * g_pages), *([cache_kb_t] * g_pages), *([cache_vb] * g_pages),
      qa_x, qb_x, mask, mask, bias_a, bias_b, kva_new_t, kb_new_t, vb_new, g_sub)


def _sample_bias(table, past, dec_seq, causal_new):
    t = jnp.arange(SAMPLE_ROWS, dtype=I32)[:, None]
    c = jnp.arange(LANES, dtype=I32)[None, :]
    far = jnp.full((SAMPLE_ROWS, LANES), MAX_DISTANCE, I32)
    last = t + LANES - c
    new = t - c
    out = []
    for dist in (far, last, new):
        b = jnp.transpose(_bias_lookup(table, dist), (2, 0, 1))
        out.append(b.reshape(table.shape[1] * SAMPLE_ROWS, LANES))
    if causal_new:
        ok = jnp.tile((new >= 0) & (c < dec_seq), (table.shape[1], 1))
        out[2] = jnp.where(ok, out[2], NEG)
    del past
    return jnp.stack(out)


def _block_diag_rows(q):
    b, t = q.shape[:2]
    q = jnp.pad(q, ((0, 0), (0, SAMPLE_ROWS - t), (0, 0), (0, 0)))
    q = jnp.transpose(q, (0, 2, 1, 3))
    eye = jnp.eye(8, dtype=q.dtype)
    return jnp.einsum('bctd,cg->bctgd', q, eye).reshape(b, 8 * SAMPLE_ROWS, 8 * HEAD_DIM)


def _tail1_kernel(n_mix, x_ref, *refs):
    o_refs = refs[:n_mix]
    w_refs = refs[n_mix:2 * n_mix]
    g_ref, wq_ref, x1_ref, qm_ref = refs[2 * n_mix:]
    x1 = x_ref[...]
    for o_ref, w_ref in zip(o_refs, w_refs):
        x1 = x1 + _dot(o_ref[...], w_ref[...])
    x1_ref[...] = x1
    hm = _rms(x1, g_ref[...]).astype(BF16)
    qm_ref[...] = (_dot(hm, wq_ref[...]) * HEAD_DIM ** -0.5).astype(BF16)


def _tail1(x, mixes, g_mem, wq16):
    n, d = x.shape
    tm = min(n, 512)
    n_mix = len(mixes)
    mw = wq16.shape[1]
    return pl.pallas_call(
        functools.partial(_tail1_kernel, n_mix),
        grid=(n // tm,),
        in_specs=([pl.BlockSpec((tm, d), lambda i: (i, 0))]
                  + [pl.BlockSpec((tm, o.shape[1]), lambda i: (i, 0)) for o, _ in mixes]
                  + [pl.BlockSpec(w.shape, lambda i: (0, 0)) for _, w in mixes]
                  + [pl.BlockSpec((1, d), lambda i: (0, 0)),
                     pl.BlockSpec(wq16.shape, lambda i: (0, 0))]),
        out_specs=[pl.BlockSpec((tm, d), lambda i: (i, 0)), pl.BlockSpec((tm, mw), lambda i: (i, 0))],
        out_shape=[jax.ShapeDtypeStruct((n, d), F32), jax.ShapeDtypeStruct((n, mw), BF16)],
        compiler_params=_cparams(("arbitrary",)),
        name="tail1",
    )(x, *[o for o, _ in mixes], *[w for _, w in mixes], g_mem, wq16)


def _tail2_kernel(qm_ref, mkv_ref, x1_ref, wo_ref, g_ref, wr_ref, br_ref, x2_ref, h_ref, route_ref):
    tq = qm_ref.shape[1]
    mem_w = N_HEADS_MEM * HEAD_DIM
    lo = lax.broadcasted_iota(I32, (tq, LANES), 1) < HEAD_DIM
    mkv = mkv_ref[0].astype(BF16)
    outs = []
    for p in range(N_HEADS_MEM // 2):
        qp = qm_ref[0, :, p * LANES:(p + 1) * LANES]
        kp = mkv[:, p * LANES:(p + 1) * LANES]
        vp = mkv[:, mem_w + p * LANES:mem_w + (p + 1) * LANES]
        pv = []
        for half in range(2):
            qh = jnp.where(lo == (half == 0), qp, jnp.zeros_like(qp))
            s = _dot_nt(qh, kp)
            e = jnp.exp(s - jnp.max(s, axis=1, keepdims=True))
            pv.append(_dot(e.astype(BF16), vp) / jnp.sum(e, axis=1, keepdims=True))
        outs.append(jnp.where(lo, pv[0], pv[1]))
    om = jnp.concatenate(outs, axis=1).astype(BF16)
    x2 = x1_ref[0] + _dot(om, wo_ref[...])
    x2_ref[0] = x2
    h = _rms(x2, g_ref[...])
    h_ref[0] = h.astype(BF16)

    lg = jnp.dot(h, wr_ref[...], preferred_element_type=F32, precision=lax.Precision.HIGHEST) + br_ref[...]
    lane = lax.broadcasted_iota(I32, (tq, LANES), 1)
    lane_f = lane.astype(F32)
    big = jnp.float32(1e9)
    is_g = lane < N_GROUPS
    m1 = jnp.max(jnp.where(is_g, lg, -jnp.inf), axis=1, keepdims=True)
    grp = jnp.min(jnp.where(is_g & (lg == m1), lane_f, big), axis=1, keepdims=True)
    p_grp = 1.0 / jnp.sum(jnp.where(is_g, jnp.exp(lg - m1), 0.0), axis=1, keepdims=True)
    e0 = N_GROUPS + grp * EXPERTS_PER_GROUP
    in_g = (lane_f >= e0) & (lane_f < e0 + EXPERTS_PER_GROUP)
    v1 = jnp.max(jnp.where(in_g, lg, -jnp.inf), axis=1, keepdims=True)
    i1 = jnp.min(jnp.where(in_g & (lg == v1), lane_f, big), axis=1, keepdims=True)
    rest = in_g & (lane_f != i1)
    v2 = jnp.max(jnp.where(rest, lg, -jnp.inf), axis=1, keepdims=True)
    i2 = jnp.min(jnp.where(rest & (lg == v2), lane_f, big), axis=1, keepdims=True)
    e2 = jnp.exp(v2 - v1)
    den = 1.0 + e2
    g1 = (1.0 / den) * p_grp
    g2 = (e2 / den) * p_grp
    route = jnp.where(lane == 0, i1 - N_GROUPS, 0.0)
    route = jnp.where(lane == 1, i2 - N_GROUPS, route)
    route = jnp.where(lane == 2, g1, route)
    route = jnp.where(lane == 3, g2, route)
    route_ref[0] = route


def _tail2(qm, mkv, x1, wo16, g_ffn, wr, br):
    b, t, d = x1.shape
    tq = min(t, 512)
    mem_w = qm.shape[2]
    n_mem = mkv.shape[1]
    return pl.pallas_call(
        _tail2_kernel,
        grid=(b, t // tq),
        in_specs=[pl.BlockSpec((1, tq, mem_w), lambda i, j: (i, j, 0)),
                  pl.BlockSpec((1, n_mem, 2 * mem_w), lambda i, j: (i, 0, 0)),
                  pl.BlockSpec((1, tq, d), lambda i, j: (i, j, 0)),
                  pl.BlockSpec(wo16.shape, lambda i, j: (0, 0)),
                  pl.BlockSpec((1, d), lambda i, j: (0, 0)),
                  pl.BlockSpec(wr.shape, lambda i, j: (0, 0)),
                  pl.BlockSpec((1, LANES), lambda i, j: (0, 0))],
        out_specs=[pl.BlockSpec((1, tq, d), lambda i, j: (i, j, 0)),
                   pl.BlockSpec((1, tq, d), lambda i, j: (i, j, 0)),
                   pl.BlockSpec((1, tq, LANES), lambda i, j: (i, j, 0))],
        out_shape=[jax.ShapeDtypeStruct((b, t, d), F32), jax.ShapeDtypeStruct((b, t, d), BF16),
                   jax.ShapeDtypeStruct((b, t, LANES), F32)],
        compiler_params=_cparams(("arbitrary", "arbitrary")),
        name="tail2",
    )(qm, mkv, x1, wo16, g_ffn, wr, br)


FLAG_FIRST, FLAG_LAST, FLAG_VALID = 1, 2, 4


def _moe_ffn_kernel(pb_ref, pc_ref, fl_ref, be_ref, h_ref, rt_ref, gate_ref, wgu_ref, wdn_ref, y_ref, acc_ref):
    del pb_ref, be_ref
    p = pl.program_id(0)
    fl = fl_ref[p]
    c = h_ref.shape[0]
    r = acc_ref.shape[0]

    @pl.when((fl & FLAG_FIRST) != 0)
    def _():
        acc_ref[...] = jnp.zeros(acc_ref.shape, F32)

    @pl.when((fl & FLAG_VALID) != 0)
    def _():
        rel = rt_ref[...] - pc_ref[p] * c
        onehot = jnp.where(rel == lax.broadcasted_iota(I32, (r, c), 1), 1.0, 0.0).astype(BF16)
        acc_ref[...] += _dot(onehot, h_ref[...])

    @pl.when((fl & FLAG_LAST) != 0)
    def _():
        gu = _dot(acc_ref[...].astype(BF16), wgu_ref[0])
        gate = gu[:, :D_EXPERT]
        up = gu[:, D_EXPERT:]
        act = gate * (1.0 / (1.0 + jnp.exp(-gate))) * up
        y_ref[...] = (_dot(act.astype(BF16), wdn_ref[0]) * gate_ref[...]).astype(BF16)


def _moe_ffn(pb, pc, flags, block_expert, h16, row_tok, row_gate, wgu16, wdn16, chunk):
    n_rows = row_tok.shape[0]
    d = h16.shape[1]
    r = EXPERT_ROWS
    grid_spec = pltpu.PrefetchScalarGridSpec(
        num_scalar_prefetch=4,
        grid=(pb.shape[0],),
        in_specs=[pl.BlockSpec((chunk, d), lambda p, pb, pc, fl, be: (pc[p], 0)),
                  pl.BlockSpec((r, 1), lambda p, pb, pc, fl, be: (pb[p], 0)),
                  pl.BlockSpec((r, 1), lambda p, pb, pc, fl, be: (pb[p], 0)),
                  pl.BlockSpec((1,) + wgu16.shape[1:], lambda p, pb, pc, fl, be: (be[pb[p]], 0, 0)),
                  pl.BlockSpec((1,) + wdn16.shape[1:], lambda p, pb, pc, fl, be: (be[pb[p]], 0, 0))],
        out_specs=pl.BlockSpec((r, d), lambda p, pb, pc, fl, be: (pb[p], 0)),
        scratch_shapes=[pltpu.VMEM((r, d), F32)])
    return pl.pallas_call(
        _moe_ffn_kernel,
        grid_spec=grid_spec,
        out_shape=jax.ShapeDtypeStruct((n_rows, d), BF16),
        compiler_params=_cparams(("arbitrary",)),
        name="moe_ffn",
    )(pb, pc, flags, block_expert, h16, row_tok.reshape(n_rows, 1), row_gate.reshape(n_rows, 1), wgu16, wdn16)


def _moe_combine_kernel(pb_ref, pc_ref, fl_ref, y_ref, rt_ref, x2_ref, g_ref, o_ref, acc_ref):
    del pb_ref
    p = pl.program_id(0)
    fl = fl_ref[p]
    c = acc_ref.shape[0]
    r = y_ref.shape[0]

    @pl.when((fl & FLAG_FIRST) != 0)
    def _():
        acc_ref[...] = jnp.zeros(acc_ref.shape, F32)

    @pl.when((fl & FLAG_VALID) != 0)
    def _():
        rel = rt_ref[0] - pc_ref[p] * c
        onehot_t = jnp.where(rel == lax.broadcasted_iota(I32, (c, r), 0), 1.0, 0.0).astype(BF16)
        acc_ref[...] += _dot(onehot_t, y_ref[...])

    @pl.when((fl & FLAG_LAST) != 0)
    def _():
        o_ref[...] = _rms(x2_ref[...] + acc_ref[...], g_ref[...])


def _moe_combine(pb, pc, flags, y_rows, row_tok, x2, g_final, chunk):
    n, d = x2.shape
    r = EXPERT_ROWS
    n_blocks = y_rows.shape[0] // r
    grid_spec = pltpu.PrefetchScalarGridSpec(
        num_scalar_prefetch=3,
        grid=(pb.shape[0],),
        in_specs=[pl.BlockSpec((r, d), lambda p, pb, pc, fl: (pb[p], 0)),
                  pl.BlockSpec((1, 1, r), lambda p, pb, pc, fl: (pb[p], 0, 0)),
                  pl.BlockSpec((chunk, d), lambda p, pb, pc, fl: (pc[p], 0)),
                  pl.BlockSpec((1, d), lambda p, pb, pc, fl: (0, 0))],
        out_specs=pl.BlockSpec((chunk, d), lambda p, pb, pc, fl: (pc[p], 0)),
        scratch_shapes=[pltpu.VMEM((chunk, d), F32)])
    return pl.pallas_call(
        _moe_combine_kernel,
        grid_spec=grid_spec,
        out_shape=jax.ShapeDtypeStruct((n, d), F32),
        compiler_params=_cparams(("arbitrary",)),
        name="moe_combine",
    )(pb, pc, flags, y_rows, row_tok.reshape(n_blocks, 1, r), x2, g_final)


def _moe_and_final(x2, h16, route, wgu16, wdn16, g_final):
    n, d = x2.shape
    n_assign = 2 * n
    experts = route[:, :2].astype(I32).reshape(n_assign)
    onehot = (experts[:, None] == jnp.arange(N_EXPERTS, dtype=I32)[None, :]).astype(I32)
    running = jnp.cumsum(onehot, axis=0)
    rank = jnp.sum((running - onehot) * onehot, axis=1)
    counts = running[-1]
    padded = (counts + EXPERT_ROWS - 1) // EXPERT_ROWS * EXPERT_ROWS
    pad_ends = jnp.cumsum(padded)
    pad_starts = pad_ends - padded
    pos = jnp.sum(onehot * pad_starts[None, :], axis=1) + rank
    n_blocks = -(-n_assign // EXPERT_ROWS) + N_EXPERTS
    n_rows = n_blocks * EXPERT_ROWS
    tok = jnp.arange(n_assign, dtype=I32) // 2
    gate_bits = lax.bitcast_convert_type(route[:, 2:4].reshape(n_assign), I32)
    rows = jnp.concatenate([jnp.full((n_rows, 1), -1, I32), jnp.zeros((n_rows, 1), I32)], axis=1)
    rows = rows.at[pos].set(jnp.stack([tok, gate_bits], axis=1))
    row_tok = rows[:, 0]
    row_gate = lax.bitcast_convert_type(rows[:, 1], F32)
    block_start = jnp.arange(n_blocks, dtype=I32) * EXPERT_ROWS
    block_expert = jnp.minimum(jnp.sum((pad_ends[None, :] <= block_start[:, None]).astype(I32), axis=1),
                               N_EXPERTS - 1).astype(I32)

    chunk = min(MOE_CHUNK, n)
    assert n % chunk == 0
    n_chunks = n // chunk
    rt2 = row_tok.reshape(n_blocks, EXPERT_ROWS)
    hi = jnp.max(rt2, axis=1)
    lo = jnp.min(jnp.where(rt2 >= 0, rt2, n), axis=1)
    lo_c = jnp.where(hi < 0, 0, lo // chunk)
    hi_c = jnp.where(hi < 0, 0, hi // chunk)
    n_ch = hi_c - lo_c + 1
    ends = jnp.cumsum(n_ch)
    starts = ends - n_ch
    total = ends[-1]
    n_pairs = N_EXPERTS * n_chunks + n_blocks
    p = jnp.arange(n_pairs, dtype=I32)
    valid = p < total
    pb = jnp.minimum(jnp.sum((ends[None, :] <= p[:, None]).astype(I32), axis=1), n_blocks - 1).astype(I32)
    pc = jnp.where(valid, lo_c[pb] + p - starts[pb], hi_c[n_blocks - 1]).astype(I32)
    flags = (jnp.where(valid & (p == starts[pb]), FLAG_FIRST, 0) + jnp.where(valid & (p == ends[pb] - 1), FLAG_LAST, 0)
             + jnp.where(valid, FLAG_VALID, 0)).astype(I32)
    y_rows = _moe_ffn(pb, pc, flags, block_expert, h16, row_tok, row_gate, wgu16, wdn16, chunk)

    order = jnp.argsort(jnp.where(valid, pc * n_blocks + pb, jnp.int32(2 ** 30)))
    pb2, pc2 = pb[order], pc[order]
    last_valid = jnp.maximum(total - 1, 0)
    pb2 = jnp.where(valid, pb2, pb2[last_valid])
    pc2 = jnp.where(valid, pc2, pc2[last_valid])
    prev_c = jnp.concatenate([jnp.full((1,), -1, I32), pc2[:-1]])
    next_c = jnp.concatenate([pc2[1:], jnp.full((1,), -1, I32)])
    flags2 = (jnp.where(valid & (pc2 != prev_c), FLAG_FIRST, 0)
              + jnp.where(valid & ((pc2 != next_c) | (p == total - 1)), FLAG_LAST, 0)
              + jnp.where(valid, FLAG_VALID, 0)).astype(I32)
    return _moe_combine(pb2, pc2, flags2, y_rows, row_tok, x2, g_final, chunk)


def kernel(x_prompt, x_sample, cache_kv_a, cache_k_idx, cache_k_b, cache_v_b, cache_mem_kv, page_table, mem_prompt, rel_bias, g_mix, w_in, g_kidx, lambda_q1, lambda_k1, lambda_q2, lambda_k2, g_subln, w_out, g_mem, g_memin, w_mem_q, w_mem_kv, w_mem_o, g_ffn, w_router1, b_router1, w_router2, b_router2, w_gate_up, w_down, g_final):
    depth = g_mix.shape[0]
    assert depth == 1
    layer = 0
    batch, seq, d = x_prompt.shape
    dec_batch, dec_seq, _ = x_sample.shape
    n_pages = page_table.shape[1]
    n_phys = cache_kv_a.shape[1]
    past = n_pages * LANES
    n_mem = mem_prompt.shape[1]
    mem_w = N_HEADS_MEM * HEAD_DIM
    assert cache_kv_a.shape[2] == LANES and dec_seq <= SAMPLE_ROWS

    table_a = rel_bias[:, :N_HEADS_A]
    table_b = rel_bias[:, N_HEADS_A:]
    lam_init = 0.8 - 0.6 * math.exp(-0.3 * layer)
    lam = (jnp.exp(jnp.sum(lambda_q1[layer] * lambda_k1[layer]).astype(F32))
           - jnp.exp(jnp.sum(lambda_q2[layer] * lambda_k2[layer]).astype(F32)) + lam_init).reshape(1)

    w = w_in[layer]
    offs = [0, 512, 1024, 1536, 2048, 2112, 2120, 2632, 3144, 3656]
    q_a, k_a, v_a, q_i, k_i, w_i, q_b, k_b, v_b = [w[:, offs[s]:offs[s + 1]] for s in range(9)]
    w_packed = jnp.concatenate(
        [q_a, k_a, v_a, q_i, k_i, k_i, jnp.pad(w_i, ((0, 0), (0, LANES - N_IDX_HEADS))), q_b, k_b, v_b],
        axis=1).astype(BF16)
    gk2 = jnp.concatenate([g_kidx[layer], g_kidx[layer]]).reshape(1, LANES)
    g_mix_l = g_mix[layer].reshape(1, d)
    w_out16 = w_out[layer].astype(BF16)
    wq16 = w_mem_q[layer].astype(BF16)
    wo16 = w_mem_o[layer].astype(BF16)
    wr = jnp.concatenate([w_router1[layer],
                          jnp.transpose(w_router2[layer], (1, 0, 2)).reshape(d, N_EXPERTS),
                          jnp.zeros((d, LANES - N_GROUPS - N_EXPERTS), F32)], axis=1)
    br = jnp.concatenate([b_router1[layer], b_router2[layer].reshape(N_EXPERTS),
                          jnp.zeros((LANES - N_GROUPS - N_EXPERTS,), F32)]).reshape(1, LANES)
    wgu16 = w_gate_up[layer].astype(BF16)
    wdn16 = w_down[layer].astype(BF16)
    g_sub = g_subln[layer].reshape(1, LANES)
    g_mem_l = g_mem[layer].reshape(1, d)
    g_ffn_l = g_ffn[layer].reshape(1, d)
    g_fin = g_final.reshape(1, d)

    n_p = batch * seq
    (qa, kva, ka16, vat, qi, ki, ki16, wi, qb, kb, vb, kb16, vbt) = _proj(
        x_prompt.reshape(n_p, d), g_mix_l, w_packed, gk2, q_scale=LOG2E)
    t_attn = min(ATTN_TILE, seq)
    o_a = _dsa_prompt(qa, qi, wi, ki16, ka16, vat, _prompt_bias_tiles(table_a, t_attn), batch, seq)
    o_b = _diff_prompt(lam, lam_init, qb, kb16, vbt, _prompt_bias_tiles(table_b, t_attn), g_sub, batch, seq)
    mkv_p = _norm_matmul(mem_prompt.reshape(batch * n_mem, d), g_memin[layer].reshape(1, d),
                         w_mem_kv[layer].astype(BF16))
    x1, qm = _tail1(x_prompt.reshape(n_p, d), [(o_a, w_out16[:A_WIDTH]), (o_b, w_out16[A_WIDTH:])],
                    g_mem_l, wq16)
    x2, h_ffn, route = _tail2(qm.reshape(batch, seq, mem_w), mkv_p.reshape(batch, n_mem, 2 * mem_w),
                              x1.reshape(batch, seq, d), wo16, g_ffn_l, wr, br)
    y_prompt = _moe_and_final(x2.reshape(n_p, d), h_ffn.reshape(n_p, d), route.reshape(n_p, LANES),
                              wgu16, wdn16, g_fin).reshape(batch, seq, d)

    n_s = dec_batch * dec_seq
    (qa_s, kva_s, _, _, qi_s, ki_s, ki16_s, wi_s, qb_s, kb_s, vb_s, kb16_s, _) = _proj(
        x_sample.reshape(n_s, d), g_mix_l, w_packed, gk2)
    kva16_s = kva_s.astype(BF16)
    pad_t = SAMPLE_ROWS - dec_seq
    qx = jnp.transpose(jnp.pad(qi_s.reshape(dec_batch, dec_seq, N_IDX_HEADS, D_IDX),
                               ((0, 0), (0, pad_t), (0, 0), (0, 0))), (0, 2, 1, 3)
                       ).reshape(dec_batch, N_IDX_HEADS * SAMPLE_ROWS, D_IDX)
    wx = jnp.transpose(jnp.pad(wi_s[:, :N_IDX_HEADS].reshape(dec_batch, dec_seq, N_IDX_HEADS),
                               ((0, 0), (0, pad_t), (0, 0))), (0, 2, 1)
                       ).reshape(dec_batch, N_IDX_HEADS * SAMPLE_ROWS, 1)
    wx = jnp.broadcast_to(wx, (dec_batch, N_IDX_HEADS * SAMPLE_ROWS, LANES))

    def new_chunk_t(a):
        a = jnp.pad(a.reshape(dec_batch, dec_seq, a.shape[1]), ((0, 0), (0, LANES - dec_seq), (0, 0)))
        return jnp.transpose(a, (0, 2, 1))

    kidx_t = jnp.transpose(cache_k_idx[layer], (0, 2, 1))
    kva_t = jnp.transpose(cache_kv_a[layer], (0, 2, 3, 4, 1)).reshape(n_phys, 2 * A_WIDTH, LANES)
    kb_t = jnp.transpose(cache_k_b[layer], (0, 2, 3, 4, 1)).reshape(n_phys, B_WIDTH, LANES)
    vb_r = cache_v_b[layer].reshape(n_phys, LANES * N_HEADS_B, 2 * HEAD_DIM)
    vb_new = jnp.pad(vb_s.reshape(dec_batch, dec_seq, N_HEADS_B, 2 * HEAD_DIM),
                     ((0, 0), (0, LANES - dec_seq), (0, 0), (0, 0))).reshape(dec_batch, LANES * N_HEADS_B, 2 * HEAD_DIM)

    g_idx = 32 if n_pages % 32 == 0 else 1
    keys = _sample_scores(page_table, kidx_t, qx, wx, new_chunk_t(ki16_s[:, :D_IDX]), dec_seq, g_idx)
    mask = _sample_select(keys, min(TOPK_MAX, (past + dec_seq) // 4))
    g_att = 16 if n_pages % 16 == 0 else 1
    o_s = _sample_attn(
        lam, lam_init, page_table, kva_t, kb_t, vb_r,
        _block_diag_rows(qa_s.reshape(dec_batch, dec_seq, N_HEADS_A, HEAD_DIM)),
        _block_diag_rows(qb_s.reshape(dec_batch, dec_seq, 2 * N_HEADS_B, HEAD_DIM)),
        mask, _sample_bias(table_a, past, dec_seq, False), _sample_bias(table_b, past, dec_seq, True),
        new_chunk_t(kva16_s), new_chunk_t(kb16_s), vb_new, g_sub, g_att)
    o_s = o_s[:, :dec_seq].reshape(n_s, A_WIDTH + B_WIDTH)
    x1_s, qm_s = _tail1(x_sample.reshape(n_s, d), [(o_s, w_out16)], g_mem_l, wq16)
    x2_s, h_s, route_s = _tail2(qm_s.reshape(dec_batch, dec_seq, mem_w),
                                cache_mem_kv[layer].reshape(dec_batch, n_mem, 2 * mem_w),
                                x1_s.reshape(dec_batch, dec_seq, d), wo16, g_ffn_l, wr, br)
    y_sample = _moe_and_final(x2_s.reshape(n_s, d), h_s.reshape(n_s, d), route_s.reshape(n_s, LANES),
                              wgu16, wdn16, g_fin).reshape(dec_batch, dec_seq, d)

    return (y_prompt, y_sample,
            kva.reshape(1, batch, seq, 2, N_HEADS_A, HEAD_DIM),
            ki[:, :D_IDX].reshape(1, batch, seq, D_IDX),
            kb.reshape(1, batch, seq, N_HEADS_B, 2, HEAD_DIM),
            vb.reshape(1, batch, seq, N_HEADS_B, 2 * HEAD_DIM),
            mkv_p.reshape(1, batch, n_mem, 2, N_HEADS_MEM, HEAD_DIM),
            kva_s.reshape(1, dec_batch, dec_seq, 2, N_HEADS_A, HEAD_DIM),
            ki_s[:, :D_IDX].reshape(1, dec_batch, dec_seq, D_IDX),
            kb_s.reshape(1, dec_batch, dec_seq, N_HEADS_B, 2, HEAD_DIM),
            vb_s.reshape(1, dec_batch, dec_seq, N_HEADS_B, 2 * HEAD_DIM))
```
